```python
import jax, jax.numpy as jnp
from jax import lax
import numpy as np

D_MODEL = 2048
BATCH = 1
SEQ = 8192
DEPTH = 2

GRID_W = 64
N_MIXERS = 2
N_SSM_LAYERS = (DEPTH + N_MIXERS - 1) // N_MIXERS
N_NA_LAYERS = DEPTH // N_MIXERS
EXPAND = 2
SSM_D_INNER = EXPAND * D_MODEL
SSM_HEAD_DIM = 64
SSM_HEADS = SSM_D_INNER // SSM_HEAD_DIM
SSM_GROUPS = 8
SSM_STATE = 128
SSM_CONV_W = 7
SSM_CHUNK = 128
SSM_CONV_DIM = SSM_D_INNER + 2 * SSM_GROUPS * SSM_STATE
SSM_IN_DIM = SSM_D_INNER + SSM_CONV_DIM + 2 * SSM_HEADS
NA_D_INNER = EXPAND * D_MODEL
NA_HEAD_DIM = 128
NA_HEADS = NA_D_INNER // NA_HEAD_DIM
NA_WIN_H = 8
NA_WIN_W = 16
NA_IN_DIM = 4 * NA_D_INNER
NORM_EPS = 1e-5
DT_MIN = 1e-3
DT_MAX = 1e-1

kernel_name = "bidir_ssd_natten_hybrid"


def rmsnorm(x, w):
    xf = x.astype(jnp.float32)
    y = xf * lax.rsqrt(jnp.mean(xf * xf, axis=-1, keepdims=True) + NORM_EPS)
    return (y * w.astype(jnp.float32)).astype(x.dtype)


def centred_depthwise_conv(u, w, b):
    c = u.shape[-1]
    pad = SSM_CONV_W // 2
    out = lax.conv_general_dilated(
        u, w[:, None, :].astype(u.dtype), window_strides=(1,), padding=[(pad, pad)],
        dimension_numbers=('NWC', 'WIO', 'NWC'), feature_group_count=c)
    return out + b.astype(u.dtype)


def ssd_scan(x, dt, A, B, C):
    b, l, h, p = x.shape
    g, n = B.shape[2], B.shape[3]
    r = h // g
    cl = SSM_CHUNK
    nc = l // cl
    f32 = jnp.float32
    xdt = (x.astype(f32) * dt[..., None]).reshape(b, nc, cl, g, r, p)
    a_cum = jnp.cumsum((dt * A).reshape(b, nc, cl, g, r), axis=2)
    Bc = B.astype(f32).reshape(b, nc, cl, g, n)
    Cc = C.astype(f32).reshape(b, nc, cl, g, n)
    mask = jnp.tril(jnp.ones((cl, cl), dtype=bool))[:, :, None, None]
    seg = a_cum[:, :, :, None] - a_cum[:, :, None, :]
    decay = jnp.exp(jnp.where(mask, seg, -jnp.inf))
    cb = jnp.einsum('bclgn,bcsgn->bclsg', Cc, Bc)
    y_diag = jnp.einsum('bclsgr,bcsgrp->bclgrp', cb[..., None] * decay, xdt)
    decay_to_end = jnp.exp(a_cum[:, :, -1:] - a_cum)
    states = jnp.einsum('bcsgn,bcsgrp->bcgrpn', Bc, decay_to_end[..., None] * xdt)
    chunk_decay = jnp.exp(a_cum[:, :, -1])

    def step(carry, inp):
        st, dec = inp
        return carry * dec[..., None, None] + st, carry

    init = jnp.zeros((b, g, r, p, n), f32)
    _, prev = lax.scan(step, init, (jnp.moveaxis(states, 1, 0), jnp.moveaxis(chunk_decay, 1, 0)))
    prev = jnp.moveaxis(prev, 0, 1)
    y_off = jnp.einsum('bclgn,bcgrpn->bclgrp', Cc, prev) * jnp.exp(a_cum)[..., None]
    return (y_diag + y_off).reshape(b, l, h, p)


def mamba2_bidir_mixer(u, w_in, conv_w, conv_b, dt_bias, A_log, D, norm_w, w_out):
    b, l, _ = u.shape
    zxbcdt = u @ w_in
    z = zxbcdt[..., :SSM_D_INNER]
    xbc = zxbcdt[..., SSM_D_INNER:SSM_D_INNER + SSM_CONV_DIM]
    dt_raw = zxbcdt[..., SSM_D_INNER + SSM_CONV_DIM:].reshape(b, l, 2, SSM_HEADS)
    xbc = jax.nn.silu(centred_depthwise_conv(xbc, conv_w, conv_b))
    gn = SSM_GROUPS * SSM_STATE
    xs = xbc[..., :SSM_D_INNER].reshape(b, l, SSM_HEADS, SSM_HEAD_DIM)
    Bm = xbc[..., SSM_D_INNER:SSM_D_INNER + gn].reshape(b, l, SSM_GROUPS, SSM_STATE)
    Cm = xbc[..., SSM_D_INNER + gn:].reshape(b, l, SSM_GROUPS, SSM_STATE)
    dt = jax.nn.softplus(dt_raw.astype(jnp.float32) + dt_bias.astype(jnp.float32))
    A = -jnp.exp(A_log.astype(jnp.float32))
    y_fwd = ssd_scan(xs, dt[:, :, 0], A[0], Bm, Cm)
    flip = lambda t: jnp.flip(t, axis=1)
    y_bwd = flip(ssd_scan(flip(xs), flip(dt[:, :, 1]), A[1], flip(Bm), flip(Cm)))
    y = y_fwd + y_bwd + D.astype(jnp.float32)[:, None] * xs.astype(jnp.float32)
    y = y.reshape(b, l, SSM_D_INNER).astype(u.dtype)
    y = rmsnorm(y * jax.nn.silu(z), norm_w)
    return y @ w_out


def neighbourhood_attention(q, k, v, rpb):
    b, l, h, dh = q.shape
    rows = l // GRID_W
    kh = min(NA_WIN_H, rows)
    kw = NA_WIN_W
    scale = dh ** -0.5
    qg = q.reshape(b, rows, GRID_W, h, dh)
    kg = k.reshape(b, rows, GRID_W, h, dh)
    vg = v.reshape(b, rows, GRID_W, h, dh)
    row_start = jnp.clip(jnp.arange(rows) - kh // 2, 0, rows - kh)
    cols = jnp.arange(GRID_W)
    col_start = jnp.clip(cols - kw // 2, 0, GRID_W - kw)
    col_idx = col_start[:, None] + jnp.arange(kw)[None, :]
    col_off = col_idx - cols[:, None] + (NA_WIN_W - 1)

    def one_row(args):
        r, q_row = args
        rs = row_start[r]
        k_band = lax.dynamic_slice_in_dim(kg, rs, kh, axis=1)
        v_band = lax.dynamic_slice_in_dim(vg, rs, kh, axis=1)
        k_win = k_band[:, :, col_idx]
        v_win = v_band[:, :, col_idx]
        s = jnp.einsum('bqhd,biqjhd->bhqij', q_row, k_win).astype(jnp.float32) * scale
        row_off = rs + jnp.arange(kh) - r + (NA_WIN_H - 1)
        bias = rpb[:, row_off][:, :, col_off]
        s = s + jnp.transpose(bias, (0, 2, 1, 3)).astype(jnp.float32)[None]
        pr = jax.nn.softmax(s.reshape(b, h, GRID_W, kh * kw), axis=-1)
        pr = pr.reshape(b, h, GRID_W, kh, kw).astype(v.dtype)
        return jnp.einsum('bhqij,biqjhd->bqhd', pr, v_win)

    out = lax.map(one_row, (jnp.arange(rows), jnp.moveaxis(qg, 1, 0)))
    return jnp.moveaxis(out, 0, 1).reshape(b, l, h, dh)


def natten_mixer(u, w_in, rpb, w_out):
    b, l, _ = u.shape
    qkvz = u @ w_in
    q, k, v, z = jnp.split(qkvz, 4, axis=-1)
    shp = (b, l, NA_HEADS, NA_HEAD_DIM)
    o = neighbourhood_attention(q.reshape(shp), k.reshape(shp), v.reshape(shp), rpb)
    o = o.reshape(b, l, NA_D_INNER) * jax.nn.silu(z)
    return o @ w_out


def setup_inputs(seed: int = 0) -> dict:
    key = jax.random.key(seed)
    ks = jax.random.split(key, 16)
    f32 = jnp.float32
    NS, NN = N_SSM_LAYERS, N_NA_LAYERS
    x = jax.random.normal(ks[0], (BATCH, SEQ, D_MODEL), f32)
    norm_w = 1.0 + 0.02 * jax.random.normal(ks[1], (DEPTH, D_MODEL), f32)
    ssm_w_in = jax.random.normal(ks[2], (NS, D_MODEL, SSM_IN_DIM), f32) * D_MODEL ** -0.5
    ssm_conv_w = jax.random.normal(ks[3], (NS, SSM_CONV_W, SSM_CONV_DIM), f32) * SSM_CONV_W ** -0.5
    ssm_conv_b = 0.02 * jax.random.normal(ks[4], (NS, SSM_CONV_DIM), f32)
    dt0 = jnp.exp(jax.random.uniform(ks[5], (NS, 2, SSM_HEADS), f32,
                                     np.log(DT_MIN), np.log(DT_MAX)))
    ssm_dt_bias = dt0 + jnp.log(-jnp.expm1(-dt0))
    ssm_A_log = jnp.log(jax.random.uniform(ks[6], (NS, 2, SSM_HEADS), f32, 1.0, 16.0))
    ssm_D = 1.0 + 0.1 * jax.random.normal(ks[7], (NS, SSM_HEADS), f32)
    ssm_norm_w = 1.0 + 0.02 * jax.random.normal(ks[8], (NS, SSM_D_INNER), f32)
    ssm_w_out = jax.random.normal(ks[9], (NS, SSM_D_INNER, D_MODEL), f32) * SSM_D_INNER ** -0.5
    na_w_in = jax.random.normal(ks[10], (NN, D_MODEL, NA_IN_DIM), f32) * D_MODEL ** -0.5
    na_rpb = 0.1 * jax.random.normal(ks[11], (NN, NA_HEADS, 2 * NA_WIN_H - 1, 2 * NA_WIN_W - 1), f32)
    na_w_out = jax.random.normal(ks[12], (NN, NA_D_INNER, D_MODEL), f32) * NA_D_INNER ** -0.5
    final_norm_w = 1.0 + 0.02 * jax.random.normal(ks[13], (D_MODEL,), f32)
    return {"x": x, "norm_w": norm_w, "ssm_w_in": ssm_w_in, "ssm_conv_w": ssm_conv_w,
            "ssm_conv_b": ssm_conv_b, "ssm_dt_bias": ssm_dt_bias, "ssm_A_log": ssm_A_log,
            "ssm_D": ssm_D, "ssm_norm_w": ssm_norm_w, "ssm_w_out": ssm_w_out,
            "na_w_in": na_w_in, "na_rpb": na_rpb, "na_w_out": na_w_out,
            "final_norm_w": final_norm_w}


def reference(x, norm_w, ssm_w_in, ssm_conv_w, ssm_conv_b, ssm_dt_bias, ssm_A_log, ssm_D,
              ssm_norm_w, ssm_w_out, na_w_in, na_rpb, na_w_out, final_norm_w):
    h = x
    for i in range(DEPTH):
        u = rmsnorm(h, norm_w[i])
        j = i // N_MIXERS
        if i % N_MIXERS == 0:
            h = h + mamba2_bidir_mixer(u, ssm_w_in[j], ssm_conv_w[j], ssm_conv_b[j], ssm_dt_bias[j],
                                       ssm_A_log[j], ssm_D[j], ssm_norm_w[j], ssm_w_out[j])
        else:
            h = h + natten_mixer(u, na_w_in[j], na_rpb[j], na_w_out[j])
    return rmsnorm(h, final_norm_w)
```

```python
import functools

import jax
import jax.numpy as jnp
from jax import lax
from jax.experimental import pallas as pl
from jax.experimental.pallas import tpu as pltpu

F32 = jnp.float32
BF16 = jnp.bfloat16

GRID_W = 64
SSM_HEAD_DIM = 64
SSM_HEADS = 64
SSM_GROUPS = 8
SSM_STATE = 128
SSM_CONV_W = 7
SSM_CHUNK = 128
SSM_D_INNER = SSM_HEADS * SSM_HEAD_DIM
SSM_CONV_DIM = SSM_D_INNER + 2 * SSM_GROUPS * SSM_STATE
NA_HEAD_DIM = 128
NA_HEADS = 32
NA_D_INNER = NA_HEADS * NA_HEAD_DIM
NA_WIN_H = 8
NA_WIN_W = 16
NORM_EPS = 1e-5

HEADS_PER_STEP = 4
SSD_LANES = HEADS_PER_STEP * SSM_HEAD_DIM
NA_Q_ROWS = 4
NA_BAND_ROWS = NA_Q_ROWS + NA_WIN_H
NEG_BIG = -1e30
VMEM_LIMIT = 56 * 1024 * 1024


def _cparams(n_axes):
    return pltpu.CompilerParams(dimension_semantics=("arbitrary",) * n_axes,
                                vmem_limit_bytes=VMEM_LIMIT)


def _rmsnorm_kernel(x_ref, w_ref, o_ref):
    x = x_ref[...]
    ms = jnp.mean(x * x, axis=-1, keepdims=True)
    o_ref[...] = (x * lax.rsqrt(ms + NORM_EPS) * w_ref[...]).astype(o_ref.dtype)


def _rmsnorm(x, w, out_dtype, tm=512):
    l, d = x.shape
    return pl.pallas_call(
        _rmsnorm_kernel,
        grid=(l // tm,),
        in_specs=[pl.BlockSpec((tm, d), lambda i: (i, 0)),
                  pl.BlockSpec((1, d), lambda i: (0, 0))],
        out_specs=pl.BlockSpec((tm, d), lambda i: (i, 0)),
        out_shape=jax.ShapeDtypeStruct((l, d), out_dtype),
        compiler_params=_cparams(1),
        name="rmsnorm",
    )(x, w.reshape(1, d))


def _gated_rmsnorm_kernel(y_ref, z_ref, w_ref, o_ref):
    z = z_ref[...].astype(F32)
    g = y_ref[...].astype(F32) * (z * jax.nn.sigmoid(z))
    ms = jnp.mean(g * g, axis=-1, keepdims=True)
    o_ref[...] = (g * lax.rsqrt(ms + NORM_EPS) * w_ref[...]).astype(o_ref.dtype)


def _gated_rmsnorm(y, zx, w, tm=256):
    l, d = y.shape
    return pl.pallas_call(
        _gated_rmsnorm_kernel,
        grid=(l // tm,),
        in_specs=[pl.BlockSpec((tm, d), lambda i: (i, 0)),
                  pl.BlockSpec((tm, d), lambda i: (i, 0)),
                  pl.BlockSpec((1, d), lambda i: (0, 0))],
        out_specs=pl.BlockSpec((tm, d), lambda i: (i, 0)),
        out_shape=jax.ShapeDtypeStruct((l, d), BF16),
        compiler_params=_cparams(1),
        name="gated_rmsnorm",
    )(y, zx, w.reshape(1, d))


def _matmul_kernel(*refs, has_scale, has_res):
    x_ref, w_ref = refs[0], refs[1]
    pos = 2
    s_ref = r_ref = None
    if has_scale:
        s_ref = refs[pos]
        pos += 1
    if has_res:
        r_ref = refs[pos]
        pos += 1
    o_ref, wbf_ref = refs[pos], refs[pos + 1]

    @pl.when(pl.program_id(1) == 0)
    def _():
        wbf_ref[...] = w_ref[...].astype(BF16)

    acc = jnp.dot(x_ref[...], wbf_ref[...], preferred_element_type=F32)
    if has_scale:
        acc = acc * s_ref[...]
    if has_res:
        acc = acc + r_ref[...]
    o_ref[...] = acc.astype(o_ref.dtype)


def _matmul(x, w, n_out, out_dtype, tm, tn, colscale=None, res=None):
    l, k = x.shape
    tm = min(tm, l)
    assert l % tm == 0 and n_out % tn == 0
    in_specs = [pl.BlockSpec((tm, k), lambda n, m: (m, 0)),
                pl.BlockSpec((k, tn), lambda n, m: (0, n))]
    args = [x, w]
    if colscale is not None:
        in_specs.append(pl.BlockSpec((1, tn), lambda n, m: (0, n)))
        args.append(colscale)
    if res is not None:
        in_specs.append(pl.BlockSpec((tm, tn), lambda n, m: (m, n)))
        args.append(res)
    return pl.pallas_call(
        functools.partial(_matmul_kernel, has_scale=colscale is not None, has_res=res is not None),
        grid=(n_out // tn, l // tm),
        in_specs=in_specs,
        out_specs=pl.BlockSpec((tm, tn), lambda n, m: (m, n)),
        out_shape=jax.ShapeDtypeStruct((l, n_out), out_dtype),
        scratch_shapes=[pltpu.VMEM((k, tn), BF16)],
        compiler_params=_cparams(2),
        name="matmul",
    )(*args)


def _split3_dot(a, m_bf16):
    hi = a.astype(BF16)
    r1 = a - hi.astype(F32)
    mid = r1.astype(BF16)
    lo = (r1 - mid.astype(F32)).astype(BF16)
    out = jnp.dot(hi, m_bf16, preferred_element_type=F32)
    out = out + jnp.dot(mid, m_bf16, preferred_element_type=F32)
    return out + jnp.dot(lo, m_bf16, preferred_element_type=F32)


def _dt_kernel(u_ref, w_ref, bias_ref, alog_ref, dt_ref, ac_ref, tot_ref):
    tm = u_ref.shape[0]
    raw = jnp.dot(u_ref[...], w_ref[...].astype(BF16), preferred_element_type=F32)
    xb = raw + bias_ref[...]
    dt = jnp.maximum(xb, 0.0) + jnp.log1p(jnp.exp(-jnp.abs(xb)))
    dt_t = dt.T
    a = dt_t * (-jnp.exp(alog_ref[...]))
    q = SSM_CHUNK
    si = lax.broadcasted_iota(jnp.int32, (q, q), 0)
    ti = lax.broadcasted_iota(jnp.int32, (q, q), 1)
    upper = jnp.where(si <= ti, 1.0, 0.0).astype(BF16)
    row = lax.broadcasted_iota(jnp.int32, (a.shape[0], q), 0)
    is_bwd = (row % (2 * HEADS_PER_STEP)) >= HEADS_PER_STEP
    dt_ref[...] = dt_t
    for c in range(tm // q):
        a_c = a[:, c * q:(c + 1) * q]
        inc = _split3_dot(a_c, upper)
        tot_ref[:, c * q:(c + 1) * q] = inc
        ac_ref[:, c * q:(c + 1) * q] = jnp.where(is_bwd, inc - a_c, inc)


def _dt_project(u, w_dt, bias, alog, tm=512):
    l, k = u.shape
    n = w_dt.shape[1]
    out = jax.ShapeDtypeStruct((n, l), F32)
    return pl.pallas_call(
        _dt_kernel,
        grid=(l // tm,),
        in_specs=[pl.BlockSpec((tm, k), lambda i: (i, 0)),
                  pl.BlockSpec((k, n), lambda i: (0, 0)),
                  pl.BlockSpec((1, n), lambda i: (0, 0)),
                  pl.BlockSpec((n, 1), lambda i: (0, 0))],
        out_specs=[pl.BlockSpec((n, tm), lambda i: (0, i))] * 3,
        out_shape=[out, out, out],
        compiler_params=_cparams(1),
        name="dt_project",
    )(u, w_dt, bias, alog)


def _conv_silu_kernel(x_ref, w_ref, b_ref, o_ref, *, tile):
    l, cw = x_ref.shape
    pad = 16
    half = SSM_CONV_W // 2
    n_tiles = l // tile
    w = w_ref[...]
    b = b_ref[...]

    def body(i, carry):
        r0 = pl.multiple_of(i * tile, tile)
        main = x_ref[pl.ds(r0, tile), :].astype(F32)
        p0 = pl.multiple_of(jnp.maximum(r0 - pad, 0), pad)
        n0 = pl.multiple_of(jnp.minimum(r0 + tile, l - pad), pad)
        prev = x_ref[pl.ds(p0, pad), :].astype(F32)
        nxt = x_ref[pl.ds(n0, pad), :].astype(F32)
        prev = jnp.where(i > 0, prev, 0.0)
        nxt = jnp.where(i < n_tiles - 1, nxt, 0.0)
        xf = jnp.concatenate([prev, main, nxt], axis=0)
        acc = jnp.zeros((tile, cw), F32) + b
        for j in range(SSM_CONV_W):
            s = pad - half + j
            acc = acc + xf[s:s + tile, :] * w[j:j + 1, :]
        o_ref[pl.ds(r0, tile), :] = (acc * jax.nn.sigmoid(acc)).astype(o_ref.dtype)
        return carry

    lax.fori_loop(0, n_tiles, body, 0)


def _conv_silu(zx, conv_w, conv_b, col0, cw=256, tile=128):
    l = zx.shape[0]
    c = conv_w.shape[1]
    off = col0 // cw
    return pl.pallas_call(
        functools.partial(_conv_silu_kernel, tile=tile),
        grid=(c // cw,),
        in_specs=[pl.BlockSpec((l, cw), lambda j: (0, off + j)),
                  pl.BlockSpec((SSM_CONV_W, cw), lambda j: (0, j)),
                  pl.BlockSpec((1, cw), lambda j: (0, j))],
        out_specs=pl.BlockSpec((l, cw), lambda j: (0, j)),
        out_shape=jax.ShapeDtypeStruct((l, c), BF16),
        compiler_params=_cparams(1),
        name="conv_silu",
    )(zx, conv_w, conv_b.reshape(1, c))


def _expand_heads(col8, off, lanehead):
    r = col8.shape[0]
    out = jnp.broadcast_to(col8[:, off:off + 1], (r, SSD_LANES))
    for k in range(1, HEADS_PER_STEP):
        out = jnp.where(lanehead == k,
                        jnp.broadcast_to(col8[:, off + k:off + k + 1], (r, SSD_LANES)), out)
    return out


def _ssd_kernel(x_ref, b_ref, c_ref, dt_ref, ac_ref, tot_ref, d_ref, y_ref, hf_ref, h_ref):
    l = x_ref.shape[0]
    q = SSM_CHUNK
    nc = l // q
    hp = HEADS_PER_STEP
    lanehead = lax.broadcasted_iota(jnp.int32, (1, SSD_LANES), 1) // SSM_HEAD_DIM
    ti = lax.broadcasted_iota(jnp.int32, (q, q), 0)
    si = lax.broadcasted_iota(jnp.int32, (q, q), 1)
    zeros_pad = jnp.zeros((q - 3 * 2 * hp, q), F32)
    contract0 = (((0,), (0,)), ((), ()))
    contract_last = (((1,), (1,)), ((), ()))

    def columns(rows):
        dtr = dt_ref[0, :, rows]
        acr = ac_ref[0, :, rows]
        totr = tot_ref[0, :, rows]
        cols = jnp.concatenate([dtr, acr, totr, zeros_pad], axis=0).T
        return dtr, acr, cols[:, 0:2 * hp], cols[:, 2 * hp:4 * hp], cols[:, 4 * hp:6 * hp]

    h_ref[...] = jnp.zeros_like(h_ref)

    def fwd_body(c, carry):
        rows = pl.ds(pl.multiple_of(c * q, q), q)
        _, _, dtc, acc, _ = columns(rows)
        a_last = acc[q - 1:q, :]
        w8 = jnp.exp(a_last - acc) * dtc
        wf = _expand_heads(w8, 0, lanehead)
        dec = _expand_heads(jnp.exp(a_last), 0, lanehead)
        xw = (x_ref[rows, :].astype(F32) * wf).astype(BF16)
        s = lax.dot_general(b_ref[rows, :], xw, contract0, preferred_element_type=F32)
        h = h_ref[...]
        hf_ref[c] = h.astype(BF16)
        h_ref[...] = h * dec + s
        return carry

    lax.fori_loop(0, nc, fwd_body, 0)

    h_ref[...] = jnp.zeros_like(h_ref)

    def bwd_body(i, carry):
        c = nc - 1 - i
        rows = pl.ds(pl.multiple_of(c * q, q), q)
        bc = b_ref[rows, :]
        cc = c_ref[rows, :]
        xc = x_ref[rows, :]
        g = lax.dot_general(cc, bc, contract_last, preferred_element_type=F32)
        dtr, acr, dtc, acc, totc = columns(rows)
        ms = []
        for k in range(hp):
            af_col, cb_col = acc[:, k:k + 1], acc[:, hp + k:hp + k + 1]
            af_row, cb_row = acr[k:k + 1, :], acr[hp + k:hp + k + 1, :]
            dtf_row, dtb_row = dtr[k:k + 1, :], dtr[hp + k:hp + k + 1, :]
            arg = jnp.where(si <= ti, af_col - af_row, cb_row - cb_col)
            wt = jnp.where(si < ti, dtf_row, jnp.where(si > ti, dtb_row, dtf_row + dtb_row))
            ms.append((g * jnp.exp(arg) * wt).astype(BF16))
        mcat = jnp.concatenate(ms, axis=1)
        xstack = jnp.concatenate(
            [jnp.where(lanehead == k, xc, jnp.zeros_like(xc)) for k in range(hp)], axis=0)
        y = jnp.dot(mcat, xstack, preferred_element_type=F32)
        scale_f = _expand_heads(jnp.exp(acc), 0, lanehead)
        y = y + jnp.dot(cc, hf_ref[c], preferred_element_type=F32) * scale_f
        t8 = totc[q - 1:q, :]
        scale_b = _expand_heads(jnp.exp(t8 - acc), hp, lanehead)
        hb = h_ref[...]
        y = y + jnp.dot(cc, hb.astype(BF16), preferred_element_type=F32) * scale_b
        xf = xc.astype(F32)
        y = y + d_ref[0] * xf
        y_ref[rows, :] = y.astype(y_ref.dtype)
        wb = _expand_heads(jnp.exp(acc) * dtc, hp, lanehead)
        xw = (xf * wb).astype(BF16)
        s = lax.dot_general(bc, xw, contract0, preferred_element_type=F32)
        h_ref[...] = hb * _expand_heads(jnp.exp(t8), hp, lanehead) + s
        return carry

    lax.fori_loop(0, nc, bwd_body, 0)


def _ssd_scan(xbc, dt_t, ac_t, tot_t, d_lanes):
    l = xbc.shape[0]
    nsteps = SSM_HEADS // HEADS_PER_STEP
    per_group = SSM_HEADS // SSM_GROUPS // HEADS_PER_STEP
    b_off = SSM_D_INNER // SSM_STATE
    c_off = b_off + SSM_GROUPS
    small = pl.BlockSpec((1, 2 * HEADS_PER_STEP, l), lambda j: (j, 0, 0))
    return pl.pallas_call(
        _ssd_kernel,
        grid=(nsteps,),
        in_specs=[pl.BlockSpec((l, SSD_LANES), lambda j: (0, j)),
                  pl.BlockSpec((l, SSM_STATE), lambda j: (0, b_off + j // per_group)),
                  pl.BlockSpec((l, SSM_STATE), lambda j: (0, c_off + j // per_group)),
                  small, small, small,
                  pl.BlockSpec((1, 1, SSD_LANES), lambda j: (j, 0, 0))],
        out_specs=pl.BlockSpec((l, SSD_LANES), lambda j: (0, j)),
        out_shape=jax.ShapeDtypeStruct((l, SSM_D_INNER), BF16),
        scratch_shapes=[pltpu.VMEM((l // SSM_CHUNK, SSM_STATE, SSD_LANES), BF16),
                        pltpu.VMEM((SSM_STATE, SSD_LANES), F32)],
        compiler_params=_cparams(1),
        name="ssd_scan",
    )(xbc, xbc, xbc, dt_t, ac_t, tot_t, d_lanes)


def _na_kernel(rpb_ref, q_ref, k_ref, v_ref, z_ref, o_ref, t_ref, bias_ref, *, rows):
    h = pl.program_id(0)
    i = pl.program_id(1)
    nblk = rows // NA_Q_ROWS
    gw = GRID_W
    band = NA_BAND_ROWS * gw
    n_dr = 2 * NA_WIN_H - 1
    n_dc = 2 * NA_WIN_W - 1

    @pl.when(i == 0)
    def _build_bias():
        qc = lax.broadcasted_iota(jnp.int32, (gw, 2 * gw), 0)
        lane = lax.broadcasted_iota(jnp.int32, (gw, 2 * gw), 1)
        kc = lane % gw
        diff = kc - qc + (NA_WIN_W - 1)
        cs = jnp.clip(qc - NA_WIN_W // 2, 0, gw - NA_WIN_W)
        col_ok = (kc >= cs) & (kc < cs + NA_WIN_W)
        for dr in range(n_dr):
            t = jnp.zeros((gw, 2 * gw), F32)
            for j in range(n_dc):
                t = jnp.where(diff == j, rpb_ref[(h * n_dr + dr) * n_dc + j], t)
            t_ref[dr] = jnp.where(col_ok, t, NEG_BIG)
        neg = jnp.full((gw, 2 * gw), NEG_BIG, F32)

        def tile(case, qr, kr):
            if case == 0:
                ok, dr = kr < NA_WIN_H, kr - qr
            elif case == 1:
                ok, dr = qr <= kr < qr + NA_WIN_H, kr - NA_WIN_H // 2 - qr
            else:
                ok, dr = kr >= NA_Q_ROWS, kr - NA_WIN_H - qr
            return t_ref[dr + NA_WIN_H - 1] if ok else neg

        for case in range(3):
            for qr in range(NA_Q_ROWS):
                for pair in range(NA_BAND_ROWS // 2):
                    both = jnp.where(lane < gw, tile(case, qr, 2 * pair), tile(case, qr, 2 * pair + 1))
                    bias_ref[case, qr * gw:(qr + 1) * gw, pair * 2 * gw:(pair + 1) * 2 * gw] = both

    def attend(case, start_tok):
        kb = k_ref[pl.ds(start_tok, band), :]
        vb = v_ref[pl.ds(start_tok, band), :]
        s = lax.dot_general(q_ref[...], kb, (((1,), (1,)), ((), ())), preferred_element_type=F32)
        s = s + bias_ref[case]
        m = jnp.max(s, axis=-1, keepdims=True)
        p = jnp.exp(s - m)
        den = jnp.sum(p, axis=-1, keepdims=True)
        o = jnp.dot(p.astype(BF16), vb, preferred_element_type=F32) / den
        z = z_ref[...].astype(F32)
        o_ref[...] = (o * (z * jax.nn.sigmoid(z))).astype(o_ref.dtype)

    @pl.when(i == 0)
    def _():
        attend(0, 0)

    @pl.when(i == nblk - 1)
    def _():
        attend(2, (rows - NA_BAND_ROWS) * gw)

    @pl.when(jnp.logical_and(i > 0, i < nblk - 1))
    def _():
        attend(1, pl.multiple_of((i * NA_Q_ROWS - NA_WIN_H // 2) * gw, gw))


def _neighbourhood_attention(qkvz, rpb):
    l = qkvz.shape[0]
    rows = l // GRID_W
    assert rows >= NA_BAND_ROWS and rows % NA_Q_ROWS == 0
    tq = NA_Q_ROWS * GRID_W
    nh = NA_HEADS
    return pl.pallas_call(
        functools.partial(_na_kernel, rows=rows),
        grid=(nh, rows // NA_Q_ROWS),
        in_specs=[pl.BlockSpec(memory_space=pltpu.SMEM),
                  pl.BlockSpec((tq, NA_HEAD_DIM), lambda h, i: (i, h)),
                  pl.BlockSpec((l, NA_HEAD_DIM), lambda h, i: (0, nh + h)),
                  pl.BlockSpec((l, NA_HEAD_DIM), lambda h, i: (0, 2 * nh + h)),
                  pl.BlockSpec((tq, NA_HEAD_DIM), lambda h, i: (i, 3 * nh + h))],
        out_specs=pl.BlockSpec((tq, NA_HEAD_DIM), lambda h, i: (i, h)),
        out_shape=jax.ShapeDtypeStruct((l, NA_D_INNER), BF16),
        scratch_shapes=[pltpu.VMEM((2 * NA_WIN_H - 1, GRID_W, 2 * GRID_W), F32),
                        pltpu.VMEM((3, tq, NA_BAND_ROWS * GRID_W), F32)],
        compiler_params=_cparams(2),
        name="neighbourhood_attention",
    )(rpb.reshape(-1), qkvz, qkvz, qkvz, qkvz)


def _ssm_layer(h, norm_w, w_in, conv_w, conv_b, dt_bias, a_log, d_skip, gnorm_w, w_out):
    u = _rmsnorm(h, norm_w, BF16)
    n_zx = SSM_D_INNER + SSM_CONV_DIM
    zx = _matmul(u, w_in, n_zx, BF16, tm=1024, tn=1024)

    hp = HEADS_PER_STEP
    heads = jnp.arange(SSM_HEADS).reshape(SSM_HEADS // hp, 1, hp)
    perm = (heads + jnp.array([0, SSM_HEADS]).reshape(1, 2, 1)).reshape(-1)
    w_dt = w_in[:, n_zx:][:, perm]
    bias = dt_bias.reshape(-1)[perm].reshape(1, -1)
    alog = a_log.reshape(-1)[perm].reshape(-1, 1)
    dt_t, ac_t, tot_t = _dt_project(u, w_dt, bias, alog)
    nsteps = SSM_HEADS // hp
    shp = (nsteps, 2 * hp, h.shape[0])

    xbc = _conv_silu(zx, conv_w, conv_b, SSM_D_INNER)
    d_lanes = jnp.repeat(d_skip, SSM_HEAD_DIM).reshape(nsteps, 1, SSD_LANES)
    y = _ssd_scan(xbc, dt_t.reshape(shp), ac_t.reshape(shp), tot_t.reshape(shp), d_lanes)
    g = _gated_rmsnorm(y, zx, gnorm_w)
    return _matmul(g, w_out, w_out.shape[1], F32, tm=1024, tn=512, res=h)


def _na_layer(h, norm_w, w_in, rpb, w_out):
    u = _rmsnorm(h, norm_w, BF16)
    n = w_in.shape[1]
    colscale = jnp.where(jnp.arange(n) < NA_D_INNER, NA_HEAD_DIM ** -0.5, 1.0).astype(F32).reshape(1, n)
    qkvz = _matmul(u, w_in, n, BF16, tm=1024, tn=1024, colscale=colscale)
    o = _neighbourhood_attention(qkvz, rpb)
    return _matmul(o, w_out, w_out.shape[1], F32, tm=1024, tn=512, res=h)


def kernel(x, norm_w, ssm_w_in, ssm_conv_w, ssm_conv_b, ssm_dt_bias, ssm_A_log, ssm_D, ssm_norm_w,
           ssm_w_out, na_w_in, na_rpb, na_w_out, final_norm_w):
    b, l, d = x.shape
    outs = []
    for bi in range(b):
        h = x[bi]
        h = _ssm_layer(h, norm_w[0], ssm_w_in[0], ssm_conv_w[0], ssm_conv_b[0], ssm_dt_bias[0],
                       ssm_A_log[0], ssm_D[0], ssm_norm_w[0], ssm_w_out[0])
        h = _na_layer(h, norm_w[1], na_w_in[0], na_rpb[0], na_w_out[0])
        outs.append(_rmsnorm(h, final_norm_w, x.dtype))
    return jnp.stack(outs, axis=0)
```

```python
import functools

import jax
import jax.numpy as jnp
from jax import lax
from jax.experimental import pallas as pl
from jax.experimental.pallas import tpu as pltpu

F32 = jnp.float32
BF16 = jnp.bfloat16

GRID_W = 64
SSM_HEAD_DIM = 64
SSM_HEADS = 64
SSM_GROUPS = 8
SSM_STATE = 128
SSM_CONV_W = 7
SSM_CHUNK = 128
SSM_D_INNER = SSM_HEADS * SSM_HEAD_DIM
SSM_CONV_DIM = SSM_D_INNER + 2 * SSM_GROUPS * SSM_STATE
NA_HEAD_DIM = 128
NA_HEADS = 32
NA_D_INNER = NA_HEADS * NA_HEAD_DIM
NA_WIN_H = 8
NA_WIN_W = 16
NORM_EPS = 1e-5

HEADS_PER_STEP = 4
SSD_LANES = HEADS_PER_STEP * SSM_HEAD_DIM
NA_Q_ROWS = 4
NA_BAND_ROWS = NA_Q_ROWS + NA_WIN_H
NA_UNROLL = 5
LOG2E = 1.4426950408889634
NEG_BIG = -1e30
VMEM_LIMIT = 56 * 1024 * 1024


def _cparams(n_axes):
    return pltpu.CompilerParams(dimension_semantics=("arbitrary",) * n_axes,
                                vmem_limit_bytes=VMEM_LIMIT)


def _rmsnorm_kernel(x_ref, w_ref, o_ref):
    x = x_ref[...]
    ms = jnp.mean(x * x, axis=-1, keepdims=True)
    o_ref[...] = (x * lax.rsqrt(ms + NORM_EPS) * w_ref[...]).astype(o_ref.dtype)


def _rmsnorm(x, w, out_dtype, tm=512):
    l, d = x.shape
    return pl.pallas_call(
        _rmsnorm_kernel,
        grid=(l // tm,),
        in_specs=[pl.BlockSpec((tm, d), lambda i: (i, 0)),
                  pl.BlockSpec((1, d), lambda i: (0, 0))],
        out_specs=pl.BlockSpec((tm, d), lambda i: (i, 0)),
        out_shape=jax.ShapeDtypeStruct((l, d), out_dtype),
        compiler_params=_cparams(1),
        name="rmsnorm",
    )(x, w.reshape(1, d))


def _gated_rmsnorm_kernel(y_ref, z_ref, w_ref, o_ref):
    z = z_ref[...].astype(F32)
    g = y_ref[...].astype(F32) * (z * jax.nn.sigmoid(z))
    ms = jnp.mean(g * g, axis=-1, keepdims=True)
    o_ref[...] = (g * lax.rsqrt(ms + NORM_EPS) * w_ref[...]).astype(o_ref.dtype)


def _gated_rmsnorm(y, zx, w, tm=256):
    l, d = y.shape
    return pl.pallas_call(
        _gated_rmsnorm_kernel,
        grid=(l // tm,),
        in_specs=[pl.BlockSpec((tm, d), lambda i: (i, 0)),
                  pl.BlockSpec((tm, d), lambda i: (i, 0)),
                  pl.BlockSpec((1, d), lambda i: (0, 0))],
        out_specs=pl.BlockSpec((tm, d), lambda i: (i, 0)),
        out_shape=jax.ShapeDtypeStruct((l, d), BF16),
        compiler_params=_cparams(1),
        name="gated_rmsnorm",
    )(y, zx, w.reshape(1, d))


def _matmul_kernel(*refs, has_scale, has_res):
    x_ref, w_ref = refs[0], refs[1]
    pos = 2
    s_ref = r_ref = None
    if has_scale:
        s_ref = refs[pos]
        pos += 1
    if has_res:
        r_ref = refs[pos]
        pos += 1
    o_ref, wbf_ref = refs[pos], refs[pos + 1]

    @pl.when(pl.program_id(1) == 0)
    def _():
        wbf_ref[...] = w_ref[...].astype(BF16)

    acc = jnp.dot(x_ref[...], wbf_ref[...], preferred_element_type=F32)
    if has_scale:
        acc = acc * s_ref[...]
    if has_res:
        acc = acc + r_ref[...]
    o_ref[...] = acc.astype(o_ref.dtype)


def _matmul(x, w, n_out, out_dtype, tm, tn, colscale=None, res=None):
    l, k = x.shape
    tm = min(tm, l)
    assert l % tm == 0 and n_out % tn == 0
    in_specs = [pl.BlockSpec((tm, k), lambda n, m: (m, 0)),
                pl.BlockSpec((k, tn), lambda n, m: (0, n))]
    args = [x, w]
    if colscale is not None:
        in_specs.append(pl.BlockSpec((1, tn), lambda n, m: (0, n)))
        args.append(colscale)
    if res is not None:
        in_specs.append(pl.BlockSpec((tm, tn), lambda n, m: (m, n)))
        args.append(res)
    return pl.pallas_call(
        functools.partial(_matmul_kernel, has_scale=colscale is not None, has_res=res is not None),
        grid=(n_out // tn, l // tm),
        in_specs=in_specs,
        out_specs=pl.BlockSpec((tm, tn), lambda n, m: (m, n)),
        out_shape=jax.ShapeDtypeStruct((l, n_out), out_dtype),
        scratch_shapes=[pltpu.VMEM((k, tn), BF16)],
        compiler_params=_cparams(2),
        name="matmul",
    )(*args)


def _split3_dot(a, m_bf16):
    hi = a.astype(BF16)
    r1 = a - hi.astype(F32)
    mid = r1.astype(BF16)
    lo = (r1 - mid.astype(F32)).astype(BF16)
    out = jnp.dot(hi, m_bf16, preferred_element_type=F32)
    out = out + jnp.dot(mid, m_bf16, preferred_element_type=F32)
    return out + jnp.dot(lo, m_bf16, preferred_element_type=F32)


def _dt_kernel(u_ref, w_ref, bias_ref, alog_ref, dt_ref, ac_ref, tot_ref, lg_ref):
    tm = u_ref.shape[0]
    raw = jnp.dot(u_ref[...], w_ref[...].astype(BF16), preferred_element_type=F32)
    xb = raw + bias_ref[...]
    dt = jnp.maximum(xb, 0.0) + jnp.log1p(jnp.exp(-jnp.abs(xb)))
    dt_t = dt.T
    a = dt_t * (-jnp.exp(alog_ref[...]) * LOG2E)
    q = SSM_CHUNK
    si = lax.broadcasted_iota(jnp.int32, (q, q), 0)
    ti = lax.broadcasted_iota(jnp.int32, (q, q), 1)
    upper = jnp.where(si <= ti, 1.0, 0.0).astype(BF16)
    row = lax.broadcasted_iota(jnp.int32, (a.shape[0], q), 0)
    is_bwd = (row % (2 * HEADS_PER_STEP)) >= HEADS_PER_STEP
    dt_ref[...] = dt_t
    log2_dt = jnp.log2(dt_t)
    for c in range(tm // q):
        cols = slice(c * q, (c + 1) * q)
        a_c = a[:, cols]
        inc = _split3_dot(a_c, upper)
        tot_ref[:, cols] = inc
        ac = jnp.where(is_bwd, inc - a_c, inc)
        ac_ref[:, cols] = ac
        lg_ref[:, cols] = log2_dt[:, cols] + jnp.where(is_bwd, ac, -ac)


def _dt_project(u, w_dt, bias, alog, tm=512):
    l, k = u.shape
    n = w_dt.shape[1]
    out = jax.ShapeDtypeStruct((n, l), F32)
    return pl.pallas_call(
        _dt_kernel,
        grid=(l // tm,),
        in_specs=[pl.BlockSpec((tm, k), lambda i: (i, 0)),
                  pl.BlockSpec((k, n), lambda i: (0, 0)),
                  pl.BlockSpec((1, n), lambda i: (0, 0)),
                  pl.BlockSpec((n, 1), lambda i: (0, 0))],
        out_specs=[pl.BlockSpec((n, tm), lambda i: (0, i))] * 4,
        out_shape=[out] * 4,
        compiler_params=_cparams(1),
        name="dt_project",
    )(u, w_dt, bias, alog)


def _conv_silu_kernel(x_ref, w_ref, b_ref, o_ref, *, tile):
    l, cw = x_ref.shape
    pad = 16
    half = SSM_CONV_W // 2
    n_tiles = l // tile
    ext = tile + 2 * pad
    w = w_ref[...]
    b = b_ref[...]
    taps = [j for j in range(SSM_CONV_W) if j != half]
    ti = lax.broadcasted_iota(jnp.int32, (len(taps) * tile, ext), 0)
    ri = lax.broadcasted_iota(jnp.int32, (len(taps) * tile, ext), 1)
    tap_of_row = jnp.zeros_like(ti)
    for n, j in enumerate(taps):
        tap_of_row = jnp.where(ti // tile == n, j, tap_of_row)
    shift = jnp.where(ri == ti % tile + pad - half + tap_of_row, 1.0, 0.0).astype(BF16)

    def shifted_rows(i):
        r0 = pl.multiple_of(i * tile, tile)
        p0 = pl.multiple_of(jnp.maximum(r0 - pad, 0), pad)
        n0 = pl.multiple_of(jnp.minimum(r0 + tile, l - pad), pad)
        zero = jnp.zeros((pad, cw), BF16)
        prev = jnp.where(i > 0, x_ref[pl.ds(p0, pad), :], zero)
        nxt = jnp.where(i < n_tiles - 1, x_ref[pl.ds(n0, pad), :], zero)
        xe = jnp.concatenate([prev, x_ref[pl.ds(r0, tile), :], nxt], axis=0)
        return jnp.dot(shift, xe, preferred_element_type=F32)

    def body(i, carry):
        shifted = shifted_rows(i)
        r0 = pl.multiple_of(i * tile, tile)
        acc = b + x_ref[pl.ds(r0, tile), :].astype(F32) * w[half:half + 1, :]
        for n, j in enumerate(taps):
            acc = acc + shifted[n * tile:(n + 1) * tile, :] * w[j:j + 1, :]
        o_ref[pl.ds(r0, tile), :] = (acc * jax.nn.sigmoid(acc)).astype(o_ref.dtype)
        return carry

    lax.fori_loop(0, n_tiles, body, 0, unroll=min(8, n_tiles))


def _conv_silu(zx, conv_w, conv_b, col0, cw=256, tile=128):
    l = zx.shape[0]
    c = conv_w.shape[1]
    off = col0 // cw
    return pl.pallas_call(
        functools.partial(_conv_silu_kernel, tile=tile),
        grid=(c // cw,),
        in_specs=[pl.BlockSpec((l, cw), lambda j: (0, off + j)),
                  pl.BlockSpec((SSM_CONV_W, cw), lambda j: (0, j)),
                  pl.BlockSpec((1, cw), lambda j: (0, j))],
        out_specs=pl.BlockSpec((l, cw), lambda j: (0, j)),
        out_shape=jax.ShapeDtypeStruct((l, c), BF16),
        compiler_params=_cparams(1),
        name="conv_silu",
    )(zx, conv_w, conv_b.reshape(1, c))


def _rows_to_lanes(r4):
    qn = r4.shape[1]
    rep = jnp.concatenate(
        [jnp.broadcast_to(r4[k:k + 1, :], (SSM_HEAD_DIM, qn)) for k in range(HEADS_PER_STEP)], axis=0)
    return rep.T


def _head_lane_vector(c4, lanehead):
    out = jnp.broadcast_to(c4[0:1, :], (1, SSD_LANES))
    for k in range(1, HEADS_PER_STEP):
        out = jnp.where(lanehead == k, jnp.broadcast_to(c4[k:k + 1, :], (1, SSD_LANES)), out)
    return out


def _ssd_kernel(x_ref, b_ref, c_ref, dt_ref, ac_ref, tot_ref, lg_ref, d_ref, y_ref, hf_ref, h_ref):
    l = x_ref.shape[0]
    q = SSM_CHUNK
    nc = l // q
    hp = HEADS_PER_STEP
    lanehead = lax.broadcasted_iota(jnp.int32, (1, SSD_LANES), 1) // SSM_HEAD_DIM
    ri = lax.broadcasted_iota(jnp.int32, (q, q), 0)
    ci = lax.broadcasted_iota(jnp.int32, (q, q), 1)
    contract0 = (((0,), (0,)), ((), ()))
    contract_last = (((1,), (1,)), ((), ()))

    h_ref[...] = jnp.zeros_like(h_ref)

    def fwd_body(c, carry):
        rows = pl.ds(pl.multiple_of(c * q, q), q)
        dtf = dt_ref[0, 0:hp, rows]
        af = ac_ref[0, 0:hp, rows]
        a_last = af[:, q - 1:q]
        wf = _rows_to_lanes(jnp.exp2(a_last - af) * dtf)
        xw = (x_ref[rows, :].astype(F32) * wf).astype(BF16)
        s = lax.dot_general(b_ref[rows, :], xw, contract0, preferred_element_type=F32)
        h = h_ref[...]
        hf_ref[c] = h.astype(BF16)
        h_ref[...] = h * _head_lane_vector(jnp.exp2(a_last), lanehead) + s
        return carry

    lax.fori_loop(0, nc, fwd_body, 0, unroll=4)

    h_ref[...] = jnp.zeros_like(h_ref)

    def bwd_body(i, carry):
        c = nc - 1 - i
        rows = pl.ds(pl.multiple_of(c * q, q), q)
        bc = b_ref[rows, :]
        cc = c_ref[rows, :]
        xc = x_ref[rows, :]
        g = lax.dot_general(cc, bc, contract_last, preferred_element_type=F32)
        dtr = dt_ref[0, :, rows]
        acr = ac_ref[0, :, rows]
        lgr = lg_ref[0, :, rows]
        ncb = -acr[hp:2 * hp]
        dsum = dtr[0:hp] + dtr[hp:2 * hp]
        ms = []
        for k in range(hp):
            col_part = jnp.where(ri <= ci, acr[k:k + 1, :], ncb[k:k + 1, :]).T
            row_part = jnp.where(ci <= ri, lgr[k:k + 1, :], lgr[hp + k:hp + k + 1, :])
            e = jnp.where(ci == ri, dsum[k:k + 1, :], jnp.exp2(col_part + row_part))
            ms.append((g * e).astype(BF16))
        mcat = jnp.concatenate(ms, axis=1)
        xstack = jnp.concatenate(
            [jnp.where(lanehead == k, xc, jnp.zeros_like(xc)) for k in range(hp)], axis=0)
        y = jnp.dot(mcat, xstack, preferred_element_type=F32)
        scale_f = _rows_to_lanes(jnp.exp2(acr[0:hp]))
        y = y + jnp.dot(cc, hf_ref[c], preferred_element_type=F32) * scale_f
        cb = acr[hp:2 * hp]
        t_last = tot_ref[0, hp:2 * hp, rows][:, q - 1:q]
        scale_b = _rows_to_lanes(jnp.exp2(t_last - cb))
        hb = h_ref[...]
        y = y + jnp.dot(cc, hb.astype(BF16), preferred_element_type=F32) * scale_b
        xf = xc.astype(F32)
        y = y + d_ref[0] * xf
        y_ref[rows, :] = y.astype(y_ref.dtype)
        wb = _rows_to_lanes(jnp.exp2(cb) * dtr[hp:2 * hp])
        xw = (xf * wb).astype(BF16)
        s = lax.dot_general(bc, xw, contract0, preferred_element_type=F32)
        h_ref[...] = hb * _head_lane_vector(jnp.exp2(t_last), lanehead) + s
        return carry

    lax.fori_loop(0, nc, bwd_body, 0, unroll=4)


def _ssd_scan(xbc, dt_t, ac_t, tot_t, lg_t, d_lanes):
    l = xbc.shape[0]
    nsteps = SSM_HEADS // HEADS_PER_STEP
    per_group = SSM_HEADS // SSM_GROUPS // HEADS_PER_STEP
    b_off = SSM_D_INNER // SSM_STATE
    c_off = b_off + SSM_GROUPS
    small = pl.BlockSpec((1, 2 * HEADS_PER_STEP, l), lambda j: (j, 0, 0))
    return pl.pallas_call(
        _ssd_kernel,
        grid=(nsteps,),
        in_specs=[pl.BlockSpec((l, SSD_LANES), lambda j: (0, j)),
                  pl.BlockSpec((l, SSM_STATE), lambda j: (0, b_off + j // per_group)),
                  pl.BlockSpec((l, SSM_STATE), lambda j: (0, c_off + j // per_group)),
                  small, small, small, small,
                  pl.BlockSpec((1, 1, SSD_LANES), lambda j: (j, 0, 0))],
        out_specs=pl.BlockSpec((l, SSD_LANES), lambda j: (0, j)),
        out_shape=jax.ShapeDtypeStruct((l, SSM_D_INNER), BF16),
        scratch_shapes=[pltpu.VMEM((l // SSM_CHUNK, SSM_STATE, SSD_LANES), BF16),
                        pltpu.VMEM((SSM_STATE, SSD_LANES), F32)],
        compiler_params=_cparams(1),
        name="ssd_scan",
    )(xbc, xbc, xbc, dt_t, ac_t, tot_t, lg_t, d_lanes)


def _na_window(case, qr):
    if case == 0:
        return 0, NA_WIN_H, -qr
    if case == 1:
        return qr, qr + NA_WIN_H, -(NA_WIN_H // 2) - qr
    return NA_Q_ROWS, NA_BAND_ROWS, -NA_WIN_H - qr


def _na_kernel(rpb_ref, q_ref, k_ref, v_ref, z_ref, o_ref, t_ref, bias_ref, *, rows):
    h = pl.program_id(0)
    nblk = rows // NA_Q_ROWS
    gw = GRID_W
    tq = NA_Q_ROWS * gw
    band = NA_BAND_ROWS * gw
    n_dr = 2 * NA_WIN_H - 1
    n_dc = 2 * NA_WIN_W - 1

    qc = lax.broadcasted_iota(jnp.int32, (gw, 2 * gw), 0)
    lane = lax.broadcasted_iota(jnp.int32, (gw, 2 * gw), 1)
    kc = lane % gw
    diff = kc - qc + (NA_WIN_W - 1)
    cs = jnp.clip(qc - NA_WIN_W // 2, 0, gw - NA_WIN_W)
    col_ok = (kc >= cs) & (kc < cs + NA_WIN_W)
    for dr in range(n_dr):
        t = jnp.zeros((gw, 2 * gw), F32)
        for j in range(n_dc):
            t = jnp.where(diff == j, rpb_ref[(h * n_dr + dr) * n_dc + j] * LOG2E, t)
        t_ref[dr] = jnp.where(col_ok, t, NEG_BIG)
    neg = jnp.full((gw, 2 * gw), NEG_BIG, F32)
    for case in range(3):
        for qr in range(NA_Q_ROWS):
            lo, hi, off = _na_window(case, qr)
            for pair in range(NA_BAND_ROWS // 2):
                halves = [t_ref[kr + off + NA_WIN_H - 1] if lo <= kr < hi else neg
                          for kr in (2 * pair, 2 * pair + 1)]
                bias_ref[case, qr * gw:(qr + 1) * gw, pair * 2 * gw:(pair + 1) * 2 * gw] = (
                    jnp.where(lane < gw, halves[0], halves[1]))

    def attend(case, q0, k0):
        kb = k_ref[pl.ds(k0, band), :]
        vb = v_ref[pl.ds(k0, band), :]
        s = lax.dot_general(q_ref[pl.ds(q0, tq), :], kb, (((1,), (1,)), ((), ())),
                            preferred_element_type=F32)
        ps, rden = [], []
        for qr in range(NA_Q_ROWS):
            lo, hi, _ = _na_window(case, qr)
            c0, c1 = (lo // 2) * 2 * gw, ((hi + 1) // 2) * 2 * gw
            sq = s[qr * gw:(qr + 1) * gw, c0:c1] + bias_ref[case, qr * gw:(qr + 1) * gw, c0:c1]
            m = jnp.max(sq, axis=-1, keepdims=True)
            p = jnp.exp2(sq - m)
            rden.append(1.0 / jnp.sum(p, axis=-1, keepdims=True))
            parts = [p.astype(BF16)]
            if c0 > 0:
                parts.insert(0, jnp.zeros((gw, c0), BF16))
            if c1 < band:
                parts.append(jnp.zeros((gw, band - c1), BF16))
            ps.append(jnp.concatenate(parts, axis=1))
        o = jnp.dot(jnp.concatenate(ps, axis=0), vb, preferred_element_type=F32)
        z = z_ref[pl.ds(q0, tq), :].astype(F32)
        o = o * jnp.concatenate(rden, axis=0) * (z * jax.nn.sigmoid(z))
        o_ref[pl.ds(q0, tq), :] = o.astype(o_ref.dtype)

    attend(0, 0, 0)

    def body(i, carry):
        q0 = pl.multiple_of(i * tq, tq)
        attend(1, q0, pl.multiple_of(q0 - (NA_WIN_H // 2) * gw, gw))
        return carry

    lax.fori_loop(1, nblk - 1, body, 0, unroll=NA_UNROLL if (nblk - 2) % NA_UNROLL == 0 else 2)
    attend(2, (nblk - 1) * tq, (rows - NA_BAND_ROWS) * gw)


def _neighbourhood_attention(qkvz, rpb):
    l = qkvz.shape[0]
    rows = l // GRID_W
    assert rows >= NA_BAND_ROWS and rows % NA_Q_ROWS == 0 and (rows // NA_Q_ROWS) % 2 == 0
    nh = NA_HEADS

    def col(part):
        return pl.BlockSpec((l, NA_HEAD_DIM), lambda h: (0, part * nh + h))

    return pl.pallas_call(
        functools.partial(_na_kernel, rows=rows),
        grid=(nh,),
        in_specs=[pl.BlockSpec(memory_space=pltpu.SMEM), col(0), col(1), col(2), col(3)],
        out_specs=col(0),
        out_shape=jax.ShapeDtypeStruct((l, NA_D_INNER), BF16),
        scratch_shapes=[pltpu.VMEM((2 * NA_WIN_H - 1, GRID_W, 2 * GRID_W), F32),
                        pltpu.VMEM((3, NA_Q_ROWS * GRID_W, NA_BAND_ROWS * GRID_W), F32)],
        compiler_params=_cparams(1),
        name="neighbourhood_attention",
    )(rpb.reshape(-1), qkvz, qkvz, qkvz, qkvz)


def _ssm_layer(h, norm_w, w_in, conv_w, conv_b, dt_bias, a_log, d_skip, gnorm_w, w_out):
    u = _rmsnorm(h, norm_w, BF16)
    n_zx = SSM_D_INNER + SSM_CONV_DIM
    zx = _matmul(u, w_in, n_zx, BF16, tm=1024, tn=1024)

    hp = HEADS_PER_STEP
    heads = jnp.arange(SSM_HEADS).reshape(SSM_HEADS // hp, 1, hp)
    perm = (heads + jnp.array([0, SSM_HEADS]).reshape(1, 2, 1)).reshape(-1)
    w_dt = w_in[:, n_zx:][:, perm]
    bias = dt_bias.reshape(-1)[perm].reshape(1, -1)
    alog = a_log.reshape(-1)[perm].reshape(-1, 1)
    dt_rows = _dt_project(u, w_dt, bias, alog)
    nsteps = SSM_HEADS // hp
    shp = (nsteps, 2 * hp, h.shape[0])

    xbc = _conv_silu(zx, conv_w, conv_b, SSM_D_INNER)
    d_lanes = jnp.repeat(d_skip, SSM_HEAD_DIM).reshape(nsteps, 1, SSD_LANES)
    y = _ssd_scan(xbc, *[r.reshape(shp) for r in dt_rows], d_lanes)
    g = _gated_rmsnorm(y, zx, gnorm_w)
    return _matmul(g, w_out, w_out.shape[1], F32, tm=1024, tn=512, res=h)


def _na_layer(h, norm_w, w_in, rpb, w_out):
    u = _rmsnorm(h, norm_w, BF16)
    n = w_in.shape[1]
    qscale = NA_HEAD_DIM ** -0.5 * LOG2E
    colscale = jnp.where(jnp.arange(n) < NA_D_INNER, qscale, 1.0).astype(F32).reshape(1, n)
    qkvz = _matmul(u, w_in, n, BF16, tm=1024, tn=1024, colscale=colscale)
    o = _neighbourhood_attention(qkvz, rpb)
    return _matmul(o, w_out, w_out.shape[1], F32, tm=1024, tn=512, res=h)


def kernel(x, norm_w, ssm_w_in, ssm_conv_w, ssm_conv_b, ssm_dt_bias, ssm_A_log, ssm_D, ssm_norm_w,
           ssm_w_out, na_w_in, na_rpb, na_w_out, final_norm_w):
    b, l, d = x.shape
    outs = []
    for bi in range(b):
        h = x[bi]
        h = _ssm_layer(h, norm_w[0], ssm_w_in[0], ssm_conv_w[0], ssm_conv_b[0], ssm_dt_bias[0],
                       ssm_A_log[0], ssm_D[0], ssm_norm_w[0], ssm_w_out[0])
        h = _na_layer(h, norm_w[1], na_w_in[0], na_rpb[0], na_w_out[0])
        outs.append(_rmsnorm(h, final_norm_w, x.dtype))
    return jnp.stack(outs, axis=0)
```

```python
import functools

import jax
import jax.numpy as jnp
from jax import lax
from jax.experimental import pallas as pl
from jax.experimental.pallas import tpu as pltpu

F32 = jnp.float32
BF16 = jnp.bfloat16

GRID_W = 64
SSM_HEAD_DIM = 64
SSM_HEADS = 64
SSM_GROUPS = 8
SSM_STATE = 128
SSM_CONV_W = 7
SSM_CHUNK = 128
SSM_D_INNER = SSM_HEADS * SSM_HEAD_DIM
SSM_CONV_DIM = SSM_D_INNER + 2 * SSM_GROUPS * SSM_STATE
NA_HEAD_DIM = 128
NA_HEADS = 32
NA_D_INNER = NA_HEADS * NA_HEAD_DIM
NA_WIN_H = 8
NA_WIN_W = 16
NORM_EPS = 1e-5

HEADS_PER_STEP = 4
SSD_LANES = HEADS_PER_STEP * SSM_HEAD_DIM
NA_Q_ROWS = 4
NA_BAND_ROWS = NA_Q_ROWS + NA_WIN_H
NA_BLOCKS_PER_TRIP = 4
LOG2E = 1.4426950408889634
NEG_BIG = -1e30
VMEM_LIMIT = 56 * 1024 * 1024


def _cparams(n_axes):
    return pltpu.CompilerParams(dimension_semantics=("arbitrary",) * n_axes,
                                vmem_limit_bytes=VMEM_LIMIT)


def _rmsnorm_kernel(x_ref, w_ref, o_ref):
    x = x_ref[...]
    ms = jnp.mean(x * x, axis=-1, keepdims=True)
    o_ref[...] = (x * lax.rsqrt(ms + NORM_EPS) * w_ref[...]).astype(o_ref.dtype)


def _rmsnorm(x, w, out_dtype, tm=512):
    l, d = x.shape
    return pl.pallas_call(
        _rmsnorm_kernel,
        grid=(l // tm,),
        in_specs=[pl.BlockSpec((tm, d), lambda i: (i, 0)),
                  pl.BlockSpec((1, d), lambda i: (0, 0))],
        out_specs=pl.BlockSpec((tm, d), lambda i: (i, 0)),
        out_shape=jax.ShapeDtypeStruct((l, d), out_dtype),
        compiler_params=_cparams(1),
        name="rmsnorm",
    )(x, w.reshape(1, d))


def _gated_rmsnorm_kernel(y_ref, z_ref, w_ref, o_ref):
    z = z_ref[...].astype(F32)
    g = y_ref[...].astype(F32) * (z * jax.nn.sigmoid(z))
    ms = jnp.mean(g * g, axis=-1, keepdims=True)
    o_ref[...] = (g * lax.rsqrt(ms + NORM_EPS) * w_ref[...]).astype(o_ref.dtype)


def _gated_rmsnorm(y, zx, w, tm=256):
    l, d = y.shape
    return pl.pallas_call(
        _gated_rmsnorm_kernel,
        grid=(l // tm,),
        in_specs=[pl.BlockSpec((tm, d), lambda i: (i, 0)),
                  pl.BlockSpec((tm, d), lambda i: (i, 0)),
                  pl.BlockSpec((1, d), lambda i: (0, 0))],
        out_specs=pl.BlockSpec((tm, d), lambda i: (i, 0)),
        out_shape=jax.ShapeDtypeStruct((l, d), BF16),
        compiler_params=_cparams(1),
        name="gated_rmsnorm",
    )(y, zx, w.reshape(1, d))


def _matmul_kernel(*refs, has_norm, has_scale, has_res):
    x_ref, w_ref = refs[0], refs[1]
    pos = 2
    g_ref = s_ref = r_ref = None
    if has_norm:
        g_ref = refs[pos]
        pos += 1
    if has_scale:
        s_ref = refs[pos]
        pos += 1
    if has_res:
        r_ref = refs[pos]
        pos += 1
    o_ref, wbf_ref = refs[pos], refs[pos + 1]

    @pl.when(pl.program_id(1) == 0)
    def _():
        w = w_ref[...]
        if has_norm:
            w = w * g_ref[...]
        wbf_ref[...] = w.astype(BF16)

    x = x_ref[...]
    acc = jnp.dot(x.astype(BF16), wbf_ref[...], preferred_element_type=F32)
    if has_norm:
        acc = acc * lax.rsqrt(jnp.mean(x * x, axis=-1, keepdims=True) + NORM_EPS)
    if has_scale:
        acc = acc * s_ref[...]
    if has_res:
        acc = acc + r_ref[...]
    o_ref[...] = acc.astype(o_ref.dtype)


def _matmul(x, w, n_out, out_dtype, tm, tn, norm_gain=None, colscale=None, res=None):
    l, k = x.shape
    tm = min(tm, l)
    assert l % tm == 0 and n_out % tn == 0
    in_specs = [pl.BlockSpec((tm, k), lambda n, m: (m, 0)),
                pl.BlockSpec((k, tn), lambda n, m: (0, n))]
    args = [x, w]
    if norm_gain is not None:
        in_specs.append(pl.BlockSpec((k, 1), lambda n, m: (0, 0)))
        args.append(norm_gain.reshape(k, 1))
    if colscale is not None:
        in_specs.append(pl.BlockSpec((1, tn), lambda n, m: (0, n)))
        args.append(colscale)
    if res is not None:
        in_specs.append(pl.BlockSpec((tm, tn), lambda n, m: (m, n)))
        args.append(res)
    return pl.pallas_call(
        functools.partial(_matmul_kernel, has_norm=norm_gain is not None,
                          has_scale=colscale is not None, has_res=res is not None),
        grid=(n_out // tn, l // tm),
        in_specs=in_specs,
        out_specs=pl.BlockSpec((tm, tn), lambda n, m: (m, n)),
        out_shape=jax.ShapeDtypeStruct((l, n_out), out_dtype),
        scratch_shapes=[pltpu.VMEM((k, tn), BF16)],
        compiler_params=_cparams(2),
        name="matmul",
    )(*args)


def _split3_dot(a, m_bf16):
    hi = a.astype(BF16)
    r1 = a - hi.astype(F32)
    mid = r1.astype(BF16)
    lo = (r1 - mid.astype(F32)).astype(BF16)
    out = jnp.dot(hi, m_bf16, preferred_element_type=F32)
    out = out + jnp.dot(mid, m_bf16, preferred_element_type=F32)
    return out + jnp.dot(lo, m_bf16, preferred_element_type=F32)


def _dt_kernel(u_ref, w_ref, bias_ref, alog_ref, dt_ref, ac_ref, tot_ref, lg_ref):
    tm = u_ref.shape[0]
    x = u_ref[...]
    raw = jnp.dot(x.astype(BF16), w_ref[...].astype(BF16), preferred_element_type=F32)
    raw = raw * lax.rsqrt(jnp.mean(x * x, axis=-1, keepdims=True) + NORM_EPS)
    xb = raw + bias_ref[...]
    dt = jnp.maximum(xb, 0.0) + jnp.log1p(jnp.exp(-jnp.abs(xb)))
    dt_t = dt.T
    a = dt_t * (-jnp.exp(alog_ref[...]) * LOG2E)
    q = SSM_CHUNK
    si = lax.broadcasted_iota(jnp.int32, (q, q), 0)
    ti = lax.broadcasted_iota(jnp.int32, (q, q), 1)
    upper = jnp.where(si <= ti, 1.0, 0.0).astype(BF16)
    row = lax.broadcasted_iota(jnp.int32, (a.shape[0], q), 0)
    is_bwd = (row % (2 * HEADS_PER_STEP)) >= HEADS_PER_STEP
    dt_ref[...] = dt_t
    log2_dt = jnp.log2(dt_t)
    for c in range(tm // q):
        cols = slice(c * q, (c + 1) * q)
        a_c = a[:, cols]
        inc = _split3_dot(a_c, upper)
        tot_ref[:, cols] = inc
        ac = jnp.where(is_bwd, inc - a_c, inc)
        ac_ref[:, cols] = ac
        lg_ref[:, cols] = log2_dt[:, cols] + jnp.where(is_bwd, ac, -ac)


def _dt_project(u, w_dt, bias, alog, tm=512):
    l, k = u.shape
    n = w_dt.shape[1]
    out = jax.ShapeDtypeStruct((n, l), F32)
    return pl.pallas_call(
        _dt_kernel,
        grid=(l // tm,),
        in_specs=[pl.BlockSpec((tm, k), lambda i: (i, 0)),
                  pl.BlockSpec((k, n), lambda i: (0, 0)),
                  pl.BlockSpec((1, n), lambda i: (0, 0)),
                  pl.BlockSpec((n, 1), lambda i: (0, 0))],
        out_specs=[pl.BlockSpec((n, tm), lambda i: (0, i))] * 4,
        out_shape=[out] * 4,
        compiler_params=_cparams(1),
        name="dt_project",
    )(u, w_dt, bias, alog)


def _conv_silu_kernel(x_ref, w_ref, b_ref, o_ref, *, tile):
    l, cw = x_ref.shape
    pad = 16
    half = SSM_CONV_W // 2
    n_tiles = l // tile
    ext = tile + 2 * pad
    w = w_ref[...]
    b = b_ref[...]
    taps = [j for j in range(SSM_CONV_W) if j != half]
    ti = lax.broadcasted_iota(jnp.int32, (len(taps) * tile, ext), 0)
    ri = lax.broadcasted_iota(jnp.int32, (len(taps) * tile, ext), 1)
    tap_of_row = jnp.zeros_like(ti)
    for n, j in enumerate(taps):
        tap_of_row = jnp.where(ti // tile == n, j, tap_of_row)
    shift = jnp.where(ri == ti % tile + pad - half + tap_of_row, 1.0, 0.0).astype(BF16)

    def shifted_rows(i):
        r0 = pl.multiple_of(i * tile, tile)
        p0 = pl.multiple_of(jnp.maximum(r0 - pad, 0), pad)
        n0 = pl.multiple_of(jnp.minimum(r0 + tile, l - pad), pad)
        zero = jnp.zeros((pad, cw), BF16)
        prev = jnp.where(i > 0, x_ref[pl.ds(p0, pad), :], zero)
        nxt = jnp.where(i < n_tiles - 1, x_ref[pl.ds(n0, pad), :], zero)
        xe = jnp.concatenate([prev, x_ref[pl.ds(r0, tile), :], nxt], axis=0)
        return jnp.dot(shift, xe, preferred_element_type=F32)

    def body(i, carry):
        shifted = shifted_rows(i)
        r0 = pl.multiple_of(i * tile, tile)
        acc = b + x_ref[pl.ds(r0, tile), :].astype(F32) * w[half:half + 1, :]
        for n, j in enumerate(taps):
            acc = acc + shifted[n * tile:(n + 1) * tile, :] * w[j:j + 1, :]
        o_ref[pl.ds(r0, tile), :] = (acc * jax.nn.sigmoid(acc)).astype(o_ref.dtype)
        return carry

    lax.fori_loop(0, n_tiles, body, 0, unroll=min(8, n_tiles))


def _conv_silu(zx, conv_w, conv_b, col0, cw=256, tile=128):
    l = zx.shape[0]
    c = conv_w.shape[1]
    off = col0 // cw
    return pl.pallas_call(
        functools.partial(_conv_silu_kernel, tile=tile),
        grid=(c // cw,),
        in_specs=[pl.BlockSpec((l, cw), lambda j: (0, off + j)),
                  pl.BlockSpec((SSM_CONV_W, cw), lambda j: (0, j)),
                  pl.BlockSpec((1, cw), lambda j: (0, j))],
        out_specs=pl.BlockSpec((l, cw), lambda j: (0, j)),
        out_shape=jax.ShapeDtypeStruct((l, c), BF16),
        compiler_params=_cparams(1),
        name="conv_silu",
    )(zx, conv_w, conv_b.reshape(1, c))


def _rows_to_lanes(r4):
    qn = r4.shape[1]
    rep = jnp.concatenate(
        [jnp.broadcast_to(r4[k:k + 1, :], (SSM_HEAD_DIM, qn)) for k in range(HEADS_PER_STEP)], axis=0)
    return rep.T


def _head_lane_vector(c4, lanehead):
    out = jnp.broadcast_to(c4[0:1, :], (1, SSD_LANES))
    for k in range(1, HEADS_PER_STEP):
        out = jnp.where(lanehead == k, jnp.broadcast_to(c4[k:k + 1, :], (1, SSD_LANES)), out)
    return out


def _ssd_kernel(x_ref, b_ref, c_ref, dt_ref, ac_ref, tot_ref, lg_ref, d_ref, y_ref, hf_ref, h_ref):
    l = x_ref.shape[0]
    q = SSM_CHUNK
    nc = l // q
    hp = HEADS_PER_STEP
    lanehead = lax.broadcasted_iota(jnp.int32, (1, SSD_LANES), 1) // SSM_HEAD_DIM
    ri = lax.broadcasted_iota(jnp.int32, (q, q), 0)
    ci = lax.broadcasted_iota(jnp.int32, (q, q), 1)
    contract0 = (((0,), (0,)), ((), ()))
    contract_last = (((1,), (1,)), ((), ()))

    h_ref[...] = jnp.zeros_like(h_ref)

    def fwd_body(c, carry):
        rows = pl.ds(pl.multiple_of(c * q, q), q)
        dtf = dt_ref[0, 0:hp, rows]
        af = ac_ref[0, 0:hp, rows]
        a_last = af[:, q - 1:q]
        wf = _rows_to_lanes(jnp.exp2(a_last - af) * dtf)
        xw = (x_ref[rows, :].astype(F32) * wf).astype(BF16)
        s = lax.dot_general(b_ref[rows, :], xw, contract0, preferred_element_type=F32)
        h = h_ref[...]
        hf_ref[c] = h.astype(BF16)
        h_ref[...] = h * _head_lane_vector(jnp.exp2(a_last), lanehead) + s
        return carry

    lax.fori_loop(0, nc, fwd_body, 0, unroll=4)

    h_ref[...] = jnp.zeros_like(h_ref)

    def bwd_body(i, carry):
        c = nc - 1 - i
        rows = pl.ds(pl.multiple_of(c * q, q), q)
        bc = b_ref[rows, :]
        cc = c_ref[rows, :]
        xc = x_ref[rows, :]
        g = lax.dot_general(cc, bc, contract_last, preferred_element_type=F32)
        dtr = dt_ref[0, :, rows]
        acr = ac_ref[0, :, rows]
        lgr = lg_ref[0, :, rows]
        ncb = -acr[hp:2 * hp]
        dsum = dtr[0:hp] + dtr[hp:2 * hp]
        ms = []
        for k in range(hp):
            col_part = jnp.where(ri <= ci, acr[k:k + 1, :], ncb[k:k + 1, :]).T
            row_part = jnp.where(ci <= ri, lgr[k:k + 1, :], lgr[hp + k:hp + k + 1, :])
            e = jnp.where(ci == ri, dsum[k:k + 1, :], jnp.exp2(col_part + row_part))
            ms.append((g * e).astype(BF16))
        mcat = jnp.concatenate(ms, axis=1)
        xstack = jnp.concatenate(
            [jnp.where(lanehead == k, xc, jnp.zeros_like(xc)) for k in range(hp)], axis=0)
        y = jnp.dot(mcat, xstack, preferred_element_type=F32)
        scale_f = _rows_to_lanes(jnp.exp2(acr[0:hp]))
        y = y + jnp.dot(cc, hf_ref[c], preferred_element_type=F32) * scale_f
        cb = acr[hp:2 * hp]
        t_last = tot_ref[0, hp:2 * hp, rows][:, q - 1:q]
        scale_b = _rows_to_lanes(jnp.exp2(t_last - cb))
        hb = h_ref[...]
        y = y + jnp.dot(cc, hb.astype(BF16), preferred_element_type=F32) * scale_b
        xf = xc.astype(F32)
        y = y + d_ref[0] * xf
        y_ref[rows, :] = y.astype(y_ref.dtype)
        wb = _rows_to_lanes(jnp.exp2(cb) * dtr[hp:2 * hp])
        xw = (xf * wb).astype(BF16)
        s = lax.dot_general(bc, xw, contract0, preferred_element_type=F32)
        h_ref[...] = hb * _head_lane_vector(jnp.exp2(t_last), lanehead) + s
        return carry

    lax.fori_loop(0, nc, bwd_body, 0, unroll=4)


def _ssd_scan(xbc, dt_t, ac_t, tot_t, lg_t, d_lanes):
    l = xbc.shape[0]
    nsteps = SSM_HEADS // HEADS_PER_STEP
    per_group = SSM_HEADS // SSM_GROUPS // HEADS_PER_STEP
    b_off = SSM_D_INNER // SSM_STATE
    c_off = b_off + SSM_GROUPS
    small = pl.BlockSpec((1, 2 * HEADS_PER_STEP, l), lambda j: (j, 0, 0))
    return pl.pallas_call(
        _ssd_kernel,
        grid=(nsteps,),
        in_specs=[pl.BlockSpec((l, SSD_LANES), lambda j: (0, j)),
                  pl.BlockSpec((l, SSM_STATE), lambda j: (0, b_off + j // per_group)),
                  pl.BlockSpec((l, SSM_STATE), lambda j: (0, c_off + j // per_group)),
                  small, small, small, small,
                  pl.BlockSpec((1, 1, SSD_LANES), lambda j: (j, 0, 0))],
        out_specs=pl.BlockSpec((l, SSD_LANES), lambda j: (0, j)),
        out_shape=jax.ShapeDtypeStruct((l, SSM_D_INNER), BF16),
        scratch_shapes=[pltpu.VMEM((l // SSM_CHUNK, SSM_STATE, SSD_LANES), BF16),
                        pltpu.VMEM((SSM_STATE, SSD_LANES), F32)],
        compiler_params=_cparams(1),
        name="ssd_scan",
    )(xbc, xbc, xbc, dt_t, ac_t, tot_t, lg_t, d_lanes)


def _na_window(case, qr):
    if case == 0:
        return 0, NA_WIN_H, -qr
    if case == 1:
        return qr, qr + NA_WIN_H, -(NA_WIN_H // 2) - qr
    return NA_Q_ROWS, NA_BAND_ROWS, -NA_WIN_H - qr


def _na_kernel(rpb_ref, q_ref, k_ref, v_ref, z_ref, o_ref, t_ref, bias_ref, *, rows):
    h = pl.program_id(0)
    nblk = rows // NA_Q_ROWS
    gw = GRID_W
    tq = NA_Q_ROWS * gw
    band = NA_BAND_ROWS * gw
    n_dr = 2 * NA_WIN_H - 1
    n_dc = 2 * NA_WIN_W - 1

    qc = lax.broadcasted_iota(jnp.int32, (gw, 2 * gw), 0)
    lane = lax.broadcasted_iota(jnp.int32, (gw, 2 * gw), 1)
    kc = lane % gw
    diff = kc - qc + (NA_WIN_W - 1)
    cs = jnp.clip(qc - NA_WIN_W // 2, 0, gw - NA_WIN_W)
    col_ok = (kc >= cs) & (kc < cs + NA_WIN_W)
    for dr in range(n_dr):
        t = jnp.zeros((gw, 2 * gw), F32)
        for j in range(n_dc):
            t = jnp.where(diff == j, rpb_ref[(h * n_dr + dr) * n_dc + j] * LOG2E, t)
        t_ref[dr] = jnp.where(col_ok, t, NEG_BIG)
    neg = jnp.full((gw, 2 * gw), NEG_BIG, F32)
    for case in range(3):
        for qr in range(NA_Q_ROWS):
            lo, hi, off = _na_window(case, qr)
            for pair in range(NA_BAND_ROWS // 2):
                halves = [t_ref[kr + off + NA_WIN_H - 1] if lo <= kr < hi else neg
                          for kr in (2 * pair, 2 * pair + 1)]
                bias_ref[case, qr * gw:(qr + 1) * gw, pair * 2 * gw:(pair + 1) * 2 * gw] = (
                    jnp.where(lane < gw, halves[0], halves[1]))

    def scores(blk):
        _, q0, k0 = blk
        return lax.dot_general(q_ref[pl.ds(q0, tq), :], k_ref[pl.ds(k0, band), :],
                               (((1,), (1,)), ((), ())), preferred_element_type=F32)

    def finish(blk, s):
        case, q0, k0 = blk
        vb = v_ref[pl.ds(k0, band), :]
        ps, rden = [], []
        for qr in range(NA_Q_ROWS):
            lo, hi, _ = _na_window(case, qr)
            c0, c1 = (lo // 2) * 2 * gw, ((hi + 1) // 2) * 2 * gw
            sq = s[qr * gw:(qr + 1) * gw, c0:c1] + bias_ref[case, qr * gw:(qr + 1) * gw, c0:c1]
            m = jnp.max(sq, axis=-1, keepdims=True)
            p = jnp.exp2(sq - m)
            rden.append(1.0 / jnp.sum(p, axis=-1, keepdims=True))
            parts = [p.astype(BF16)]
            if c0 > 0:
                parts.insert(0, jnp.zeros((gw, c0), BF16))
            if c1 < band:
                parts.append(jnp.zeros((gw, band - c1), BF16))
            ps.append(jnp.concatenate(parts, axis=1))
        o = jnp.dot(jnp.concatenate(ps, axis=0), vb, preferred_element_type=F32)
        z = z_ref[pl.ds(q0, tq), :].astype(F32)
        o = o * jnp.concatenate(rden, axis=0) * (z * jax.nn.sigmoid(z))
        o_ref[pl.ds(q0, tq), :] = o.astype(o_ref.dtype)

    def run_blocks(blocks):
        s = scores(blocks[0])
        for j, blk in enumerate(blocks):
            s_next = scores(blocks[j + 1]) if j + 1 < len(blocks) else None
            finish(blk, s)
            s = s_next

    def interior(i):
        q0 = i * tq
        if not isinstance(i, int):
            q0 = pl.multiple_of(q0, tq)
        return 1, q0, q0 - (NA_WIN_H // 2) * gw

    u = NA_BLOCKS_PER_TRIP
    top = (0, 0, 0)
    bottom = (2, (nblk - 1) * tq, (rows - NA_BAND_ROWS) * gw)
    run_blocks([top] + [interior(i) for i in range(1, u)])

    def trip(t, carry):
        run_blocks([interior(t * u + j) for j in range(u)])
        return carry

    lax.fori_loop(1, nblk // u - 1, trip, 0)
    run_blocks([interior(i) for i in range(nblk - u, nblk - 1)] + [bottom])


def _neighbourhood_attention(qkvz, rpb):
    l = qkvz.shape[0]
    rows = l // GRID_W
    nblk = rows // NA_Q_ROWS
    assert rows % NA_Q_ROWS == 0 and nblk % NA_BLOCKS_PER_TRIP == 0 and nblk >= 2 * NA_BLOCKS_PER_TRIP
    nh = NA_HEADS

    def col(part):
        return pl.BlockSpec((l, NA_HEAD_DIM), lambda h: (0, part * nh + h))

    return pl.pallas_call(
        functools.partial(_na_kernel, rows=rows),
        grid=(nh,),
        in_specs=[pl.BlockSpec(memory_space=pltpu.SMEM), col(0), col(1), col(2), col(3)],
        out_specs=col(0),
        out_shape=jax.ShapeDtypeStruct((l, NA_D_INNER), BF16),
        scratch_shapes=[pltpu.VMEM((2 * NA_WIN_H - 1, GRID_W, 2 * GRID_W), F32),
                        pltpu.VMEM((3, NA_Q_ROWS * GRID_W, NA_BAND_ROWS * GRID_W), F32)],
        compiler_params=_cparams(1),
        name="neighbourhood_attention",
    )(rpb.reshape(-1), qkvz, qkvz, qkvz, qkvz)


def _ssm_layer(h, norm_w, w_in, conv_w, conv_b, dt_bias, a_log, d_skip, gnorm_w, w_out):
    n_zx = SSM_D_INNER + SSM_CONV_DIM
    zx = _matmul(h, w_in, n_zx, BF16, tm=1024, tn=1024, norm_gain=norm_w)

    hp = HEADS_PER_STEP
    heads = jnp.arange(SSM_HEADS).reshape(SSM_HEADS // hp, 1, hp)
    perm = (heads + jnp.array([0, SSM_HEADS]).reshape(1, 2, 1)).reshape(-1)
    w_dt = w_in[:, n_zx:][:, perm] * norm_w[:, None]
    bias = dt_bias.reshape(-1)[perm].reshape(1, -1)
    alog = a_log.reshape(-1)[perm].reshape(-1, 1)
    dt_rows = _dt_project(h, w_dt, bias, alog)
    nsteps = SSM_HEADS // hp
    shp = (nsteps, 2 * hp, h.shape[0])

    xbc = _conv_silu(zx, conv_w, conv_b, SSM_D_INNER)
    d_lanes = jnp.repeat(d_skip, SSM_HEAD_DIM).reshape(nsteps, 1, SSD_LANES)
    y = _ssd_scan(xbc, *[r.reshape(shp) for r in dt_rows], d_lanes)
    g = _gated_rmsnorm(y, zx, gnorm_w)
    return _matmul(g, w_out, w_out.shape[1], F32, tm=1024, tn=512, res=h)


def _na_layer(h, norm_w, w_in, rpb, w_out):
    n = w_in.shape[1]
    qscale = NA_HEAD_DIM ** -0.5 * LOG2E
    colscale = jnp.where(jnp.arange(n) < NA_D_INNER, qscale, 1.0).astype(F32).reshape(1, n)
    qkvz = _matmul(h, w_in, n, BF16, tm=1024, tn=1024, norm_gain=norm_w, colscale=colscale)
    o = _neighbourhood_attention(qkvz, rpb)
    return _matmul(o, w_out, w_out.shape[1], F32, tm=1024, tn=512, res=h)


def kernel(x, norm_w, ssm_w_in, ssm_conv_w, ssm_conv_b, ssm_dt_bias, ssm_A_log, ssm_D, ssm_norm_w,
           ssm_w_out, na_w_in, na_rpb, na_w_out, final_norm_w):
    b, l, d = x.shape
    outs = []
    for bi in range(b):
        h = x[bi]
        h = _ssm_layer(h, norm_w[0], ssm_w_in[0], ssm_conv_w[0], ssm_conv_b[0], ssm_dt_bias[0],
                       ssm_A_log[0], ssm_D[0], ssm_norm_w[0], ssm_w_out[0])
        h = _na_layer(h, norm_w[1], na_w_in[0], na_rpb[0], na_w_out[0])
        outs.append(_rmsnorm(h, final_norm_w, x.dtype))
    return jnp.stack(outs, axis=0)
```

```python
import functools

import jax
import jax.numpy as jnp
from jax import lax
from jax.experimental import pallas as pl
from jax.experimental.pallas import tpu as pltpu

F32 = jnp.float32
BF16 = jnp.bfloat16

GRID_W = 64
SSM_HEAD_DIM = 64
SSM_HEADS = 64
SSM_GROUPS = 8
SSM_STATE = 128
SSM_CONV_W = 7
SSM_CHUNK = 128
SSM_D_INNER = SSM_HEADS * SSM_HEAD_DIM
SSM_CONV_DIM = SSM_D_INNER + 2 * SSM_GROUPS * SSM_STATE
NA_HEAD_DIM = 128
NA_HEADS = 32
NA_D_INNER = NA_HEADS * NA_HEAD_DIM
NA_WIN_H = 8
NA_WIN_W = 16
NORM_EPS = 1e-5

HEADS_PER_STEP = 4
SSD_LANES = HEADS_PER_STEP * SSM_HEAD_DIM
SSD_CHUNKS_PER_TRIP = 8
NA_Q_ROWS = 4
NA_BAND_ROWS = NA_Q_ROWS + NA_WIN_H
NA_BLOCKS_PER_TRIP = 4
LOG2E = 1.4426950408889634
NEG_BIG = -1e30
VMEM_LIMIT = 56 * 1024 * 1024


def _cparams(n_axes):
    return pltpu.CompilerParams(dimension_semantics=("arbitrary",) * n_axes,
                                vmem_limit_bytes=VMEM_LIMIT)


def _rmsnorm_kernel(x_ref, w_ref, o_ref):
    x = x_ref[...]
    ms = jnp.mean(x * x, axis=-1, keepdims=True)
    o_ref[...] = (x * lax.rsqrt(ms + NORM_EPS) * w_ref[...]).astype(o_ref.dtype)


def _rmsnorm(x, w, out_dtype, tm=512):
    l, d = x.shape
    return pl.pallas_call(
        _rmsnorm_kernel,
        grid=(l // tm,),
        in_specs=[pl.BlockSpec((tm, d), lambda i: (i, 0)),
                  pl.BlockSpec((1, d), lambda i: (0, 0))],
        out_specs=pl.BlockSpec((tm, d), lambda i: (i, 0)),
        out_shape=jax.ShapeDtypeStruct((l, d), out_dtype),
        compiler_params=_cparams(1),
        name="rmsnorm",
    )(x, w.reshape(1, d))


def _gated_rmsnorm_kernel(y_ref, z_ref, w_ref, o_ref):
    z = z_ref[...].astype(F32)
    g = y_ref[...].astype(F32) * (z * jax.nn.sigmoid(z))
    ms = jnp.mean(g * g, axis=-1, keepdims=True)
    o_ref[...] = (g * lax.rsqrt(ms + NORM_EPS) * w_ref[...]).astype(o_ref.dtype)


def _gated_rmsnorm(y, zx, w, tm=256):
    l, d = y.shape
    return pl.pallas_call(
        _gated_rmsnorm_kernel,
        grid=(l // tm,),
        in_specs=[pl.BlockSpec((tm, d), lambda i: (i, 0)),
                  pl.BlockSpec((tm, d), lambda i: (i, 0)),
                  pl.BlockSpec((1, d), lambda i: (0, 0))],
        out_specs=pl.BlockSpec((tm, d), lambda i: (i, 0)),
        out_shape=jax.ShapeDtypeStruct((l, d), BF16),
        compiler_params=_cparams(1),
        name="gated_rmsnorm",
    )(y, zx, w.reshape(1, d))


def _matmul_kernel(*refs, has_scale, has_res):
    x_ref, w_ref = refs[0], refs[1]
    pos = 2
    s_ref = r_ref = None
    if has_scale:
        s_ref = refs[pos]
        pos += 1
    if has_res:
        r_ref = refs[pos]
        pos += 1
    o_ref, wbf_ref = refs[pos], refs[pos + 1]

    @pl.when(pl.program_id(1) == 0)
    def _():
        wbf_ref[...] = w_ref[...].astype(BF16)

    acc = jnp.dot(x_ref[...], wbf_ref[...], preferred_element_type=F32)
    if has_scale:
        acc = acc * s_ref[...]
    if has_res:
        acc = acc + r_ref[...]
    o_ref[...] = acc.astype(o_ref.dtype)


def _matmul(x, w, n_out, out_dtype, tm, tn, colscale=None, res=None):
    l, k = x.shape
    tm = min(tm, l)
    assert l % tm == 0 and n_out % tn == 0
    in_specs = [pl.BlockSpec((tm, k), lambda n, m: (m, 0)),
                pl.BlockSpec((k, tn), lambda n, m: (0, n))]
    args = [x, w]
    if colscale is not None:
        in_specs.append(pl.BlockSpec((1, tn), lambda n, m: (0, n)))
        args.append(colscale)
    if res is not None:
        in_specs.append(pl.BlockSpec((tm, tn), lambda n, m: (m, n)))
        args.append(res)
    return pl.pallas_call(
        functools.partial(_matmul_kernel, has_scale=colscale is not None, has_res=res is not None),
        grid=(n_out // tn, l // tm),
        in_specs=in_specs,
        out_specs=pl.BlockSpec((tm, tn), lambda n, m: (m, n)),
        out_shape=jax.ShapeDtypeStruct((l, n_out), out_dtype),
        scratch_shapes=[pltpu.VMEM((k, tn), BF16)],
        compiler_params=_cparams(2),
        name="matmul",
    )(*args)


def _split3_dot(a, m_bf16):
    hi = a.astype(BF16)
    r1 = a - hi.astype(F32)
    mid = r1.astype(BF16)
    lo = (r1 - mid.astype(F32)).astype(BF16)
    out = jnp.dot(hi, m_bf16, preferred_element_type=F32)
    out = out + jnp.dot(mid, m_bf16, preferred_element_type=F32)
    return out + jnp.dot(lo, m_bf16, preferred_element_type=F32)


def _dt_kernel(u_ref, w_ref, bias_ref, alog_ref, dt_ref, ac_ref, tot_ref, lg_ref):
    tm = u_ref.shape[0]
    raw = jnp.dot(u_ref[...], w_ref[...].astype(BF16), preferred_element_type=F32)
    xb = raw + bias_ref[...]
    dt = jnp.maximum(xb, 0.0) + jnp.log1p(jnp.exp(-jnp.abs(xb)))
    dt_t = dt.T
    a = dt_t * (-jnp.exp(alog_ref[...]) * LOG2E)
    q = SSM_CHUNK
    si = lax.broadcasted_iota(jnp.int32, (q, q), 0)
    ti = lax.broadcasted_iota(jnp.int32, (q, q), 1)
    upper = jnp.where(si <= ti, 1.0, 0.0).astype(BF16)
    row = lax.broadcasted_iota(jnp.int32, (a.shape[0], q), 0)
    is_bwd = (row % (2 * HEADS_PER_STEP)) >= HEADS_PER_STEP
    dt_ref[...] = dt_t
    log2_dt = jnp.log2(dt_t)
    for c in range(tm // q):
        cols = slice(c * q, (c + 1) * q)
        a_c = a[:, cols]
        inc = _split3_dot(a_c, upper)
        tot_ref[:, cols] = inc
        ac = jnp.where(is_bwd, inc - a_c, inc)
        ac_ref[:, cols] = ac
        lg_ref[:, cols] = log2_dt[:, cols] + jnp.where(is_bwd, ac, -ac)


def _dt_project(u, w_dt, bias, alog, tm=512):
    l, k = u.shape
    n = w_dt.shape[1]
    out = jax.ShapeDtypeStruct((n, l), F32)
    return pl.pallas_call(
        _dt_kernel,
        grid=(l // tm,),
        in_specs=[pl.BlockSpec((tm, k), lambda i: (i, 0)),
                  pl.BlockSpec((k, n), lambda i: (0, 0)),
                  pl.BlockSpec((1, n), lambda i: (0, 0)),
                  pl.BlockSpec((n, 1), lambda i: (0, 0))],
        out_specs=[pl.BlockSpec((n, tm), lambda i: (0, i))] * 4,
        out_shape=[out] * 4,
        compiler_params=_cparams(1),
        name="dt_project",
    )(u, w_dt, bias, alog)


def _conv_silu_kernel(x_ref, w_ref, b_ref, o_ref, *, tile):
    l, cw = x_ref.shape
    pad = 16
    half = SSM_CONV_W // 2
    n_tiles = l // tile
    ext = tile + 2 * pad
    w = w_ref[...]
    b = b_ref[...]
    taps = [j for j in range(SSM_CONV_W) if j != half]
    ti = lax.broadcasted_iota(jnp.int32, (len(taps) * tile, ext), 0)
    ri = lax.broadcasted_iota(jnp.int32, (len(taps) * tile, ext), 1)
    tap_of_row = jnp.zeros_like(ti)
    for n, j in enumerate(taps):
        tap_of_row = jnp.where(ti // tile == n, j, tap_of_row)
    shift = jnp.where(ri == ti % tile + pad - half + tap_of_row, 1.0, 0.0).astype(BF16)

    def shifted_rows(i):
        r0 = pl.multiple_of(i * tile, tile)
        p0 = pl.multiple_of(jnp.maximum(r0 - pad, 0), pad)
        n0 = pl.multiple_of(jnp.minimum(r0 + tile, l - pad), pad)
        zero = jnp.zeros((pad, cw), BF16)
        prev = jnp.where(i > 0, x_ref[pl.ds(p0, pad), :], zero)
        nxt = jnp.where(i < n_tiles - 1, x_ref[pl.ds(n0, pad), :], zero)
        xe = jnp.concatenate([prev, x_ref[pl.ds(r0, tile), :], nxt], axis=0)
        return jnp.dot(shift, xe, preferred_element_type=F32)

    def body(i, carry):
        shifted = shifted_rows(i)
        r0 = pl.multiple_of(i * tile, tile)
        acc = b + x_ref[pl.ds(r0, tile), :].astype(F32) * w[half:half + 1, :]
        for n, j in enumerate(taps):
            acc = acc + shifted[n * tile:(n + 1) * tile, :] * w[j:j + 1, :]
        o_ref[pl.ds(r0, tile), :] = (acc * jax.nn.sigmoid(acc)).astype(o_ref.dtype)
        return carry

    lax.fori_loop(0, n_tiles, body, 0, unroll=min(8, n_tiles))


def _conv_silu(zx, conv_w, conv_b, col0, cw=256, tile=128):
    l = zx.shape[0]
    c = conv_w.shape[1]
    off = col0 // cw
    return pl.pallas_call(
        functools.partial(_conv_silu_kernel, tile=tile),
        grid=(c // cw,),
        in_specs=[pl.BlockSpec((l, cw), lambda j: (0, off + j)),
                  pl.BlockSpec((SSM_CONV_W, cw), lambda j: (0, j)),
                  pl.BlockSpec((1, cw), lambda j: (0, j))],
        out_specs=pl.BlockSpec((l, cw), lambda j: (0, j)),
        out_shape=jax.ShapeDtypeStruct((l, c), BF16),
        compiler_params=_cparams(1),
        name="conv_silu",
    )(zx, conv_w, conv_b.reshape(1, c))


def _head_selector(first_rows):
    n = len(first_rows)
    shape = (3 * 2 * HEADS_PER_STEP * n, SSD_LANES * n)
    row = lax.broadcasted_iota(jnp.int32, shape, 0)
    lane = lax.broadcasted_iota(jnp.int32, shape, 1)
    first = jnp.zeros(shape, jnp.int32)
    for i, f in enumerate(first_rows):
        first = jnp.where(lane // SSD_LANES == i, f, first)
    hit = (row // (3 * 2 * HEADS_PER_STEP) == lane // SSD_LANES) & (
        row % (2 * HEADS_PER_STEP) == (lane % SSD_LANES) // SSM_HEAD_DIM + first)
    return jnp.where(hit, 1.0, 0.0).astype(BF16)


def _rows_to_lanes(blocks, selector):
    pieces = []
    for r8 in blocks:
        hi = r8.astype(BF16).astype(F32)
        r1 = r8 - hi
        mid = r1.astype(BF16).astype(F32)
        pieces += [hi, mid, r1 - mid]
    lhs = jnp.concatenate(pieces, axis=0).astype(BF16)
    return lax.dot_general(lhs, selector, (((0,), (0,)), ((), ())), preferred_element_type=F32)


def _head_lane_vector(c4, lanehead):
    out = jnp.broadcast_to(c4[0:1, :], (1, SSD_LANES))
    for k in range(1, HEADS_PER_STEP):
        out = jnp.where(lanehead == k, jnp.broadcast_to(c4[k:k + 1, :], (1, SSD_LANES)), out)
    return out


def _ssd_kernel(x_ref, b_ref, c_ref, dt_ref, ac_ref, tot_ref, lg_ref, d_ref, y_ref, hf_ref, h_ref):
    l = x_ref.shape[0]
    q = SSM_CHUNK
    nc = l // q
    hp = HEADS_PER_STEP
    lanehead = lax.broadcasted_iota(jnp.int32, (1, SSD_LANES), 1) // SSM_HEAD_DIM
    ri = lax.broadcasted_iota(jnp.int32, (q, q), 0)
    ci = lax.broadcasted_iota(jnp.int32, (q, q), 1)
    contract0 = (((0,), (0,)), ((), ()))
    contract_last = (((1,), (1,)), ((), ()))
    sel_f = _head_selector([0])
    sel_fbb = _head_selector([0, hp, hp])
    u = SSD_CHUNKS_PER_TRIP

    def chunk_rows(c):
        return pl.ds(pl.multiple_of(c * q, q), q)


    h_ref[...] = jnp.zeros_like(h_ref)

    def fwd_trip(t, carry):
        chunks = [t * u + j for j in range(u)]
        a_lasts, wfs, ss = [], [], []
        for c in chunks:
            rows = chunk_rows(c)
            acr = ac_ref[0, :, rows]
            a_lasts.append(acr[:, q - 1:q])
            wfs.append(_rows_to_lanes([jnp.exp2(a_lasts[-1] - acr) * dt_ref[0, :, rows]], sel_f))
        for c, wf in zip(chunks, wfs):
            rows = chunk_rows(c)
            xw = (x_ref[rows, :].astype(F32) * wf).astype(BF16)
            ss.append(lax.dot_general(b_ref[rows, :], xw, contract0, preferred_element_type=F32))
        h = h_ref[...]
        for c, a_last, s in zip(chunks, a_lasts, ss):
            hf_ref[c] = h.astype(BF16)
            h = h * _head_lane_vector(jnp.exp2(a_last[0:hp]), lanehead) + s
        h_ref[...] = h
        return carry

    lax.fori_loop(0, nc // u, fwd_trip, 0)

    h_ref[...] = jnp.zeros_like(h_ref)

    def stage_a(c):
        rows = chunk_rows(c)
        bc, cc = b_ref[rows, :], c_ref[rows, :]
        dtr, acr = dt_ref[0, :, rows], ac_ref[0, :, rows]
        t_last = tot_ref[0, :, rows][:, q - 1:q]
        g = lax.dot_general(cc, bc, contract_last, preferred_element_type=F32)
        scales = _rows_to_lanes([jnp.exp2(acr), jnp.exp2(t_last - acr), jnp.exp2(acr) * dtr], sel_fbb)
        yoff_f = jnp.dot(cc, hf_ref[c], preferred_element_type=F32) * scales[:, 0:SSD_LANES]
        return dict(rows=rows, bc=bc, cc=cc, dtr=dtr, acr=acr, t_last=t_last, g=g, yoff_f=yoff_f,
                    scale_b=scales[:, SSD_LANES:2 * SSD_LANES], wb=scales[:, 2 * SSD_LANES:])

    def stage_b(d):
        xw = (x_ref[d["rows"], :].astype(F32) * d["wb"]).astype(BF16)
        d["s"] = lax.dot_general(d["bc"], xw, contract0, preferred_element_type=F32)

    def stage_c(d, hb):
        chb = jnp.dot(d["cc"], hb.astype(BF16), preferred_element_type=F32)
        hb = hb * _head_lane_vector(jnp.exp2(d["t_last"][hp:2 * hp]), lanehead) + d["s"]
        dtr, acr, g = d["dtr"], d["acr"], d["g"]
        lgr = lg_ref[0, :, d["rows"]]
        ncb = -acr[hp:2 * hp]
        dsum = dtr[0:hp] + dtr[hp:2 * hp]
        ms = []
        for k in range(hp):
            col_part = jnp.where(ri <= ci, acr[k:k + 1, :], ncb[k:k + 1, :]).T
            row_part = jnp.where(ci <= ri, lgr[k:k + 1, :], lgr[hp + k:hp + k + 1, :])
            e = jnp.where(ci == ri, dsum[k:k + 1, :], jnp.exp2(col_part + row_part))
            ms.append((g * e).astype(BF16))
        mcat = jnp.concatenate(ms, axis=1)
        xc = x_ref[d["rows"], :]
        xstack = jnp.concatenate(
            [jnp.where(lanehead == k, xc, jnp.zeros_like(xc)) for k in range(hp)], axis=0)
        y = jnp.dot(mcat, xstack, preferred_element_type=F32)
        y = y + d["yoff_f"] + chb * d["scale_b"] + d_ref[0] * xc.astype(F32)
        y_ref[d["rows"], :] = y.astype(y_ref.dtype)
        return hb

    def bwd_trip(t, carry):
        chunks = [nc - 1 - (t * u + j) for j in range(u)]
        st = [None] * u
        st[0] = stage_a(chunks[0])
        if u > 1:
            st[1] = stage_a(chunks[1])
        stage_b(st[0])
        hb = h_ref[...]
        for j in range(u):
            if j + 2 < u:
                st[j + 2] = stage_a(chunks[j + 2])
            if j + 1 < u:
                stage_b(st[j + 1])
            hb = stage_c(st[j], hb)
            st[j] = None
        h_ref[...] = hb
        return carry

    lax.fori_loop(0, nc // u, bwd_trip, 0)


def _ssd_scan(xbc, dt_t, ac_t, tot_t, lg_t, d_lanes):
    l = xbc.shape[0]
    nsteps = SSM_HEADS // HEADS_PER_STEP
    per_group = SSM_HEADS // SSM_GROUPS // HEADS_PER_STEP
    b_off = SSM_D_INNER // SSM_STATE
    c_off = b_off + SSM_GROUPS
    small = pl.BlockSpec((1, 2 * HEADS_PER_STEP, l), lambda j: (j, 0, 0))
    return pl.pallas_call(
        _ssd_kernel,
        grid=(nsteps,),
        in_specs=[pl.BlockSpec((l, SSD_LANES), lambda j: (0, j)),
                  pl.BlockSpec((l, SSM_STATE), lambda j: (0, b_off + j // per_group)),
                  pl.BlockSpec((l, SSM_STATE), lambda j: (0, c_off + j // per_group)),
                  small, small, small, small,
                  pl.BlockSpec((1, 1, SSD_LANES), lambda j: (j, 0, 0))],
        out_specs=pl.BlockSpec((l, SSD_LANES), lambda j: (0, j)),
        out_shape=jax.ShapeDtypeStruct((l, SSM_D_INNER), BF16),
        scratch_shapes=[pltpu.VMEM((l // SSM_CHUNK, SSM_STATE, SSD_LANES), BF16),
                        pltpu.VMEM((SSM_STATE, SSD_LANES), F32)],
        compiler_params=_cparams(1),
        name="ssd_scan",
    )(xbc, xbc, xbc, dt_t, ac_t, tot_t, lg_t, d_lanes)


def _na_window(case, qr):
    if case == 0:
        return 0, NA_WIN_H, -qr
    if case == 1:
        return qr, qr + NA_WIN_H, -(NA_WIN_H // 2) - qr
    return NA_Q_ROWS, NA_BAND_ROWS, -NA_WIN_H - qr


def _na_kernel(rpb_ref, q_ref, k_ref, v_ref, z_ref, o_ref, t_ref, bias_ref, *, rows):
    h = pl.program_id(0)
    nblk = rows // NA_Q_ROWS
    gw = GRID_W
    tq = NA_Q_ROWS * gw
    band = NA_BAND_ROWS * gw
    n_dr = 2 * NA_WIN_H - 1
    n_dc = 2 * NA_WIN_W - 1

    qc = lax.broadcasted_iota(jnp.int32, (gw, 2 * gw), 0)
    lane = lax.broadcasted_iota(jnp.int32, (gw, 2 * gw), 1)
    kc = lane % gw
    diff = kc - qc + (NA_WIN_W - 1)
    cs = jnp.clip(qc - NA_WIN_W // 2, 0, gw - NA_WIN_W)
    col_ok = (kc >= cs) & (kc < cs + NA_WIN_W)
    for dr in range(n_dr):
        t = jnp.zeros((gw, 2 * gw), F32)
        for j in range(n_dc):
            t = jnp.where(diff == j, rpb_ref[(h * n_dr + dr) * n_dc + j] * LOG2E, t)
        t_ref[dr] = jnp.where(col_ok, t, NEG_BIG)
    neg = jnp.full((gw, 2 * gw), NEG_BIG, F32)
    for case in range(3):
        for qr in range(NA_Q_ROWS):
            lo, hi, off = _na_window(case, qr)
            for pair in range(NA_BAND_ROWS // 2):
                halves = [t_ref[kr + off + NA_WIN_H - 1] if lo <= kr < hi else neg
                          for kr in (2 * pair, 2 * pair + 1)]
                bias_ref[case, qr * gw:(qr + 1) * gw, pair * 2 * gw:(pair + 1) * 2 * gw] = (
                    jnp.where(lane < gw, halves[0], halves[1]))

    def scores(blk):
        _, q0, k0 = blk
        return lax.dot_general(q_ref[pl.ds(q0, tq), :], k_ref[pl.ds(k0, band), :],
                               (((1,), (1,)), ((), ())), preferred_element_type=F32)

    def finish(blk, s):
        case, q0, k0 = blk
        vb = v_ref[pl.ds(k0, band), :]
        ps, rden = [], []
        for qr in range(NA_Q_ROWS):
            lo, hi, _ = _na_window(case, qr)
            c0, c1 = (lo // 2) * 2 * gw, ((hi + 1) // 2) * 2 * gw
            sq = s[qr * gw:(qr + 1) * gw, c0:c1] + bias_ref[case, qr * gw:(qr + 1) * gw, c0:c1]
            m = jnp.max(sq, axis=-1, keepdims=True)
            p = jnp.exp2(sq - m)
            rden.append(1.0 / jnp.sum(p, axis=-1, keepdims=True))
            parts = [p.astype(BF16)]
            if c0 > 0:
                parts.insert(0, jnp.zeros((gw, c0), BF16))
            if c1 < band:
                parts.append(jnp.zeros((gw, band - c1), BF16))
            ps.append(jnp.concatenate(parts, axis=1))
        o = jnp.dot(jnp.concatenate(ps, axis=0), vb, preferred_element_type=F32)
        z = z_ref[pl.ds(q0, tq), :].astype(F32)
        o = o * jnp.concatenate(rden, axis=0) * (z * jax.nn.sigmoid(z))
        o_ref[pl.ds(q0, tq), :] = o.astype(o_ref.dtype)

    def run_blocks(blocks):
        s = scores(blocks[0])
        for j, blk in enumerate(blocks):
            s_next = scores(blocks[j + 1]) if j + 1 < len(blocks) else None
            finish(blk, s)
            s = s_next

    def interior(i):
        q0 = i * tq
        if not isinstance(i, int):
            q0 = pl.multiple_of(q0, tq)
        return 1, q0, q0 - (NA_WIN_H // 2) * gw

    u = NA_BLOCKS_PER_TRIP
    top = (0, 0, 0)
    bottom = (2, (nblk - 1) * tq, (rows - NA_BAND_ROWS) * gw)
    run_blocks([top] + [interior(i) for i in range(1, u)])

    def trip(t, carry):
        run_blocks([interior(t * u + j) for j in range(u)])
        return carry

    lax.fori_loop(1, nblk // u - 1, trip, 0)
    run_blocks([interior(i) for i in range(nblk - u, nblk - 1)] + [bottom])


def _neighbourhood_attention(qkvz, rpb):
    l = qkvz.shape[0]
    rows = l // GRID_W
    nblk = rows // NA_Q_ROWS
    assert rows % NA_Q_ROWS == 0 and nblk % NA_BLOCKS_PER_TRIP == 0 and nblk >= 2 * NA_BLOCKS_PER_TRIP
    nh = NA_HEADS

    def col(part):
        return pl.BlockSpec((l, NA_HEAD_DIM), lambda h: (0, part * nh + h))

    return pl.pallas_call(
        functools.partial(_na_kernel, rows=rows),
        grid=(nh,),
        in_specs=[pl.BlockSpec(memory_space=pltpu.SMEM), col(0), col(1), col(2), col(3)],
        out_specs=col(0),
        out_shape=jax.ShapeDtypeStruct((l, NA_D_INNER), BF16),
        scratch_shapes=[pltpu.VMEM((2 * NA_WIN_H - 1, GRID_W, 2 * GRID_W), F32),
                        pltpu.VMEM((3, NA_Q_ROWS * GRID_W, NA_BAND_ROWS * GRID_W), F32)],
        compiler_params=_cparams(1),
        name="neighbourhood_attention",
    )(rpb.reshape(-1), qkvz, qkvz, qkvz, qkvz)


def _ssm_layer(h, norm_w, w_in, conv_w, conv_b, dt_bias, a_log, d_skip, gnorm_w, w_out):
    u = _rmsnorm(h, norm_w, BF16)
    n_zx = SSM_D_INNER + SSM_CONV_DIM
    zx = _matmul(u, w_in, n_zx, BF16, tm=1024, tn=1024)

    hp = HEADS_PER_STEP
    heads = jnp.arange(SSM_HEADS).reshape(SSM_HEADS // hp, 1, hp)
    perm = (heads + jnp.array([0, SSM_HEADS]).reshape(1, 2, 1)).reshape(-1)
    w_dt = w_in[:, n_zx:][:, perm]
    bias = dt_bias.reshape(-1)[perm].reshape(1, -1)
    alog = a_log.reshape(-1)[perm].reshape(-1, 1)
    dt_rows = _dt_project(u, w_dt, bias, alog)
    nsteps = SSM_HEADS // hp
    shp = (nsteps, 2 * hp, h.shape[0])

    xbc = _conv_silu(zx, conv_w, conv_b, SSM_D_INNER)
    d_lanes = jnp.repeat(d_skip, SSM_HEAD_DIM).reshape(nsteps, 1, SSD_LANES)
    y = _ssd_scan(xbc, *[r.reshape(shp) for r in dt_rows], d_lanes)
    g = _gated_rmsnorm(y, zx, gnorm_w)
    return _matmul(g, w_out, w_out.shape[1], F32, tm=1024, tn=512, res=h)


def _na_layer(h, norm_w, w_in, rpb, w_out):
    u = _rmsnorm(h, norm_w, BF16)
    n = w_in.shape[1]
    qscale = NA_HEAD_DIM ** -0.5 * LOG2E
    colscale = jnp.where(jnp.arange(n) < NA_D_INNER, qscale, 1.0).astype(F32).reshape(1, n)
    qkvz = _matmul(u, w_in, n, BF16, tm=1024, tn=1024, colscale=colscale)
    o = _neighbourhood_attention(qkvz, rpb)
    return _matmul(o, w_out, w_out.shape[1], F32, tm=1024, tn=512, res=h)


def kernel(x, norm_w, ssm_w_in, ssm_conv_w, ssm_conv_b, ssm_dt_bias, ssm_A_log, ssm_D, ssm_norm_w,
           ssm_w_out, na_w_in, na_rpb, na_w_out, final_norm_w):
    b, l, d = x.shape
    outs = []
    for bi in range(b):
        h = x[bi]
        h = _ssm_layer(h, norm_w[0], ssm_w_in[0], ssm_conv_w[0], ssm_conv_b[0], ssm_dt_bias[0],
                       ssm_A_log[0], ssm_D[0], ssm_norm_w[0], ssm_w_out[0])
        h = _na_layer(h, norm_w[1], na_w_in[0], na_rpb[0], na_w_out[0])
        outs.append(_rmsnorm(h, final_norm_w, x.dtype))
    return jnp.stack(outs, axis=0)
```

```python
import functools

import jax
import jax.numpy as jnp
from jax import lax
from jax.experimental import pallas as pl
from jax.experimental.pallas import tpu as pltpu

F32 = jnp.float32
BF16 = jnp.bfloat16

GRID_W = 64
SSM_HEAD_DIM = 64
SSM_HEADS = 64
SSM_GROUPS = 8
SSM_STATE = 128
SSM_CONV_W = 7
SSM_CHUNK = 128
SSM_D_INNER = SSM_HEADS * SSM_HEAD_DIM
SSM_CONV_DIM = SSM_D_INNER + 2 * SSM_GROUPS * SSM_STATE
NA_HEAD_DIM = 128
NA_HEADS = 32
NA_D_INNER = NA_HEADS * NA_HEAD_DIM
NA_WIN_H = 8
NA_WIN_W = 16
NORM_EPS = 1e-5

HEADS_PER_STEP = 4
SSD_LANES = HEADS_PER_STEP * SSM_HEAD_DIM
SSD_CHUNKS_PER_TRIP = 8
NA_Q_ROWS = 4
NA_BAND_ROWS = NA_Q_ROWS + NA_WIN_H
NA_BLOCKS_PER_TRIP = 8
LOG2E = 1.4426950408889634
NEG_BIG = -1e30
VMEM_LIMIT = 56 * 1024 * 1024


def _cparams(n_axes):
    return pltpu.CompilerParams(dimension_semantics=("arbitrary",) * n_axes,
                                vmem_limit_bytes=VMEM_LIMIT)


def _rmsnorm_kernel(x_ref, w_ref, o_ref):
    x = x_ref[...]
    ms = jnp.mean(x * x, axis=-1, keepdims=True)
    o_ref[...] = (x * lax.rsqrt(ms + NORM_EPS) * w_ref[...]).astype(o_ref.dtype)


def _rmsnorm(x, w, out_dtype, tm=512):
    l, d = x.shape
    return pl.pallas_call(
        _rmsnorm_kernel,
        grid=(l // tm,),
        in_specs=[pl.BlockSpec((tm, d), lambda i: (i, 0)),
                  pl.BlockSpec((1, d), lambda i: (0, 0))],
        out_specs=pl.BlockSpec((tm, d), lambda i: (i, 0)),
        out_shape=jax.ShapeDtypeStruct((l, d), out_dtype),
        compiler_params=_cparams(1),
        name="rmsnorm",
    )(x, w.reshape(1, d))


def _gated_rmsnorm_kernel(y_ref, z_ref, w_ref, o_ref):
    z = z_ref[...].astype(F32)
    g = y_ref[...].astype(F32) * (z * jax.nn.sigmoid(z))
    ms = jnp.mean(g * g, axis=-1, keepdims=True)
    o_ref[...] = (g * lax.rsqrt(ms + NORM_EPS) * w_ref[...]).astype(o_ref.dtype)


def _gated_rmsnorm(y, zx, w, tm=256):
    l, d = y.shape
    return pl.pallas_call(
        _gated_rmsnorm_kernel,
        grid=(l // tm,),
        in_specs=[pl.BlockSpec((tm, d), lambda i: (i, 0)),
                  pl.BlockSpec((tm, d), lambda i: (i, 0)),
                  pl.BlockSpec((1, d), lambda i: (0, 0))],
        out_specs=pl.BlockSpec((tm, d), lambda i: (i, 0)),
        out_shape=jax.ShapeDtypeStruct((l, d), BF16),
        compiler_params=_cparams(1),
        name="gated_rmsnorm",
    )(y, zx, w.reshape(1, d))


def _matmul_kernel(*refs, has_scale, has_res):
    x_ref, w_ref = refs[0], refs[1]
    pos = 2
    s_ref = r_ref = None
    if has_scale:
        s_ref = refs[pos]
        pos += 1
    if has_res:
        r_ref = refs[pos]
        pos += 1
    o_ref, wbf_ref = refs[pos], refs[pos + 1]

    @pl.when(pl.program_id(1) == 0)
    def _():
        wbf_ref[...] = w_ref[...].astype(BF16)

    acc = jnp.dot(x_ref[...], wbf_ref[...], preferred_element_type=F32)
    if has_scale:
        acc = acc * s_ref[...]
    if has_res:
        acc = acc + r_ref[...]
    o_ref[...] = acc.astype(o_ref.dtype)


def _matmul(x, w, n_out, out_dtype, tm, tn, colscale=None, res=None):
    l, k = x.shape
    tm = min(tm, l)
    assert l % tm == 0 and n_out % tn == 0
    in_specs = [pl.BlockSpec((tm, k), lambda n, m: (m, 0)),
                pl.BlockSpec((k, tn), lambda n, m: (0, n))]
    args = [x, w]
    if colscale is not None:
        in_specs.append(pl.BlockSpec((1, tn), lambda n, m: (0, n)))
        args.append(colscale)
    if res is not None:
        in_specs.append(pl.BlockSpec((tm, tn), lambda n, m: (m, n)))
        args.append(res)
    return pl.pallas_call(
        functools.partial(_matmul_kernel, has_scale=colscale is not None, has_res=res is not None),
        grid=(n_out // tn, l // tm),
        in_specs=in_specs,
        out_specs=pl.BlockSpec((tm, tn), lambda n, m: (m, n)),
        out_shape=jax.ShapeDtypeStruct((l, n_out), out_dtype),
        scratch_shapes=[pltpu.VMEM((k, tn), BF16)],
        compiler_params=_cparams(2),
        name="matmul",
    )(*args)


def _split3_dot(a, m_bf16):
    hi = a.astype(BF16)
    r1 = a - hi.astype(F32)
    mid = r1.astype(BF16)
    lo = (r1 - mid.astype(F32)).astype(BF16)
    out = jnp.dot(hi, m_bf16, preferred_element_type=F32)
    out = out + jnp.dot(mid, m_bf16, preferred_element_type=F32)
    return out + jnp.dot(lo, m_bf16, preferred_element_type=F32)


def _dt_kernel(u_ref, w_ref, bias_ref, alog_ref, dt_ref, ac_ref, tot_ref, lg_ref):
    tm = u_ref.shape[0]
    raw = jnp.dot(u_ref[...], w_ref[...].astype(BF16), preferred_element_type=F32)
    xb = raw + bias_ref[...]
    dt = jnp.maximum(xb, 0.0) + jnp.log1p(jnp.exp(-jnp.abs(xb)))
    dt_t = dt.T
    a = dt_t * (-jnp.exp(alog_ref[...]) * LOG2E)
    q = SSM_CHUNK
    si = lax.broadcasted_iota(jnp.int32, (q, q), 0)
    ti = lax.broadcasted_iota(jnp.int32, (q, q), 1)
    upper = jnp.where(si <= ti, 1.0, 0.0).astype(BF16)
    row = lax.broadcasted_iota(jnp.int32, (a.shape[0], q), 0)
    is_bwd = (row % (2 * HEADS_PER_STEP)) >= HEADS_PER_STEP
    dt_ref[...] = dt_t
    log2_dt = jnp.log2(dt_t)
    for c in range(tm // q):
        cols = slice(c * q, (c + 1) * q)
        a_c = a[:, cols]
        inc = _split3_dot(a_c, upper)
        tot_ref[:, cols] = inc
        ac = jnp.where(is_bwd, inc - a_c, inc)
        ac_ref[:, cols] = ac
        lg_ref[:, cols] = log2_dt[:, cols] + jnp.where(is_bwd, ac, -ac)


def _dt_project(u, w_dt, bias, alog, tm=512):
    l, k = u.shape
    n = w_dt.shape[1]
    out = jax.ShapeDtypeStruct((n, l), F32)
    return pl.pallas_call(
        _dt_kernel,
        grid=(l // tm,),
        in_specs=[pl.BlockSpec((tm, k), lambda i: (i, 0)),
                  pl.BlockSpec((k, n), lambda i: (0, 0)),
                  pl.BlockSpec((1, n), lambda i: (0, 0)),
                  pl.BlockSpec((n, 1), lambda i: (0, 0))],
        out_specs=[pl.BlockSpec((n, tm), lambda i: (0, i))] * 4,
        out_shape=[out] * 4,
        compiler_params=_cparams(1),
        name="dt_project",
    )(u, w_dt, bias, alog)


def _conv_silu_kernel(x_ref, w_ref, b_ref, o_ref, *, tile):
    l, cw = x_ref.shape
    pad = 16
    half = SSM_CONV_W // 2
    n_tiles = l // tile
    ext = tile + 2 * pad
    w = w_ref[...]
    b = b_ref[...]
    taps = [j for j in range(SSM_CONV_W) if j != half]
    ti = lax.broadcasted_iota(jnp.int32, (len(taps) * tile, ext), 0)
    ri = lax.broadcasted_iota(jnp.int32, (len(taps) * tile, ext), 1)
    tap_of_row = jnp.zeros_like(ti)
    for n, j in enumerate(taps):
        tap_of_row = jnp.where(ti // tile == n, j, tap_of_row)
    shift = jnp.where(ri == ti % tile + pad - half + tap_of_row, 1.0, 0.0).astype(BF16)

    def shifted_rows(i):
        r0 = pl.multiple_of(i * tile, tile)
        p0 = pl.multiple_of(jnp.maximum(r0 - pad, 0), pad)
        n0 = pl.multiple_of(jnp.minimum(r0 + tile, l - pad), pad)
        zero = jnp.zeros((pad, cw), BF16)
        prev = jnp.where(i > 0, x_ref[pl.ds(p0, pad), :], zero)
        nxt = jnp.where(i < n_tiles - 1, x_ref[pl.ds(n0, pad), :], zero)
        xe = jnp.concatenate([prev, x_ref[pl.ds(r0, tile), :], nxt], axis=0)
        return jnp.dot(shift, xe, preferred_element_type=F32)

    def body(i, carry):
        shifted = shifted_rows(i)
        r0 = pl.multiple_of(i * tile, tile)
        acc = b + x_ref[pl.ds(r0, tile), :].astype(F32) * w[half:half + 1, :]
        for n, j in enumerate(taps):
            acc = acc + shifted[n * tile:(n + 1) * tile, :] * w[j:j + 1, :]
        o_ref[pl.ds(r0, tile), :] = (acc * jax.nn.sigmoid(acc)).astype(o_ref.dtype)
        return carry

    lax.fori_loop(0, n_tiles, body, 0, unroll=min(16, n_tiles))


def _conv_silu(zx, conv_w, conv_b, col0, cw=256, tile=128):
    l = zx.shape[0]
    c = conv_w.shape[1]
    off = col0 // cw
    return pl.pallas_call(
        functools.partial(_conv_silu_kernel, tile=tile),
        grid=(c // cw,),
        in_specs=[pl.BlockSpec((l, cw), lambda j: (0, off + j)),
                  pl.BlockSpec((SSM_CONV_W, cw), lambda j: (0, j)),
                  pl.BlockSpec((1, cw), lambda j: (0, j))],
        out_specs=pl.BlockSpec((l, cw), lambda j: (0, j)),
        out_shape=jax.ShapeDtypeStruct((l, c), BF16),
        compiler_params=_cparams(1),
        name="conv_silu",
    )(zx, conv_w, conv_b.reshape(1, c))


def _head_selector(first_rows):
    n = len(first_rows)
    shape = (3 * 2 * HEADS_PER_STEP * n, SSD_LANES * n)
    row = lax.broadcasted_iota(jnp.int32, shape, 0)
    lane = lax.broadcasted_iota(jnp.int32, shape, 1)
    first = jnp.zeros(shape, jnp.int32)
    for i, f in enumerate(first_rows):
        first = jnp.where(lane // SSD_LANES == i, f, first)
    hit = (row // (3 * 2 * HEADS_PER_STEP) == lane // SSD_LANES) & (
        row % (2 * HEADS_PER_STEP) == (lane % SSD_LANES) // SSM_HEAD_DIM + first)
    return jnp.where(hit, 1.0, 0.0).astype(BF16)


def _rows_to_lanes(blocks, selector):
    pieces = []
    for r8 in blocks:
        hi = r8.astype(BF16).astype(F32)
        r1 = r8 - hi
        mid = r1.astype(BF16).astype(F32)
        pieces += [hi, mid, r1 - mid]
    lhs = jnp.concatenate(pieces, axis=0).astype(BF16)
    return lax.dot_general(lhs, selector, (((0,), (0,)), ((), ())), preferred_element_type=F32)


def _head_lane_vector(c4, lanehead):
    out = jnp.broadcast_to(c4[0:1, :], (1, SSD_LANES))
    for k in range(1, HEADS_PER_STEP):
        out = jnp.where(lanehead == k, jnp.broadcast_to(c4[k:k + 1, :], (1, SSD_LANES)), out)
    return out


def _ssd_kernel(x_ref, b_ref, c_ref, dt_ref, ac_ref, tot_ref, lg_ref, d_ref, y_ref, hf_ref, h_ref):
    l = x_ref.shape[0]
    q = SSM_CHUNK
    nc = l // q
    hp = HEADS_PER_STEP
    lanehead = lax.broadcasted_iota(jnp.int32, (1, SSD_LANES), 1) // SSM_HEAD_DIM
    ri = lax.broadcasted_iota(jnp.int32, (q, q), 0)
    ci = lax.broadcasted_iota(jnp.int32, (q, q), 1)
    contract0 = (((0,), (0,)), ((), ()))
    contract_last = (((1,), (1,)), ((), ()))
    sel_f = _head_selector([0])
    sel_fbb = _head_selector([0, hp, hp])
    u = SSD_CHUNKS_PER_TRIP

    def chunk_rows(c):
        return pl.ds(pl.multiple_of(c * q, q), q)


    h_ref[...] = jnp.zeros_like(h_ref)

    def fwd_trip(t, carry):
        chunks = [t * u + j for j in range(u)]
        a_lasts, wfs, ss = [], [], []
        for c in chunks:
            rows = chunk_rows(c)
            acr = ac_ref[0, :, rows]
            a_lasts.append(acr[:, q - 1:q])
            wfs.append(_rows_to_lanes([jnp.exp2(a_lasts[-1] - acr) * dt_ref[0, :, rows]], sel_f))
        for c, wf in zip(chunks, wfs):
            rows = chunk_rows(c)
            xw = (x_ref[rows, :].astype(F32) * wf).astype(BF16)
            ss.append(lax.dot_general(b_ref[rows, :], xw, contract0, preferred_element_type=F32))
        h = h_ref[...]
        for c, a_last, s in zip(chunks, a_lasts, ss):
            hf_ref[c] = h.astype(BF16)
            h = h * _head_lane_vector(jnp.exp2(a_last[0:hp]), lanehead) + s
        h_ref[...] = h
        return carry

    lax.fori_loop(0, nc // u, fwd_trip, 0)

    h_ref[...] = jnp.zeros_like(h_ref)

    def stage_a(c):
        rows = chunk_rows(c)
        bc, cc = b_ref[rows, :], c_ref[rows, :]
        dtr, acr = dt_ref[0, :, rows], ac_ref[0, :, rows]
        t_last = tot_ref[0, :, rows][:, q - 1:q]
        g = lax.dot_general(cc, bc, contract_last, preferred_element_type=F32)
        scales = _rows_to_lanes([jnp.exp2(acr), jnp.exp2(t_last - acr), jnp.exp2(acr) * dtr], sel_fbb)
        yoff_f = jnp.dot(cc, hf_ref[c], preferred_element_type=F32) * scales[:, 0:SSD_LANES]
        return dict(rows=rows, bc=bc, cc=cc, dtr=dtr, acr=acr, t_last=t_last, g=g, yoff_f=yoff_f,
                    scale_b=scales[:, SSD_LANES:2 * SSD_LANES], wb=scales[:, 2 * SSD_LANES:])

    def stage_b(d):
        xw = (x_ref[d["rows"], :].astype(F32) * d["wb"]).astype(BF16)
        d["s"] = lax.dot_general(d["bc"], xw, contract0, preferred_element_type=F32)

    def stage_c(d, hb):
        chb = jnp.dot(d["cc"], hb.astype(BF16), preferred_element_type=F32)
        hb = hb * _head_lane_vector(jnp.exp2(d["t_last"][hp:2 * hp]), lanehead) + d["s"]
        dtr, acr, g = d["dtr"], d["acr"], d["g"]
        lgr = lg_ref[0, :, d["rows"]]
        ncb = -acr[hp:2 * hp]
        dsum = dtr[0:hp] + dtr[hp:2 * hp]
        ms = []
        for k in range(hp):
            col_part = jnp.where(ri <= ci, acr[k:k + 1, :], ncb[k:k + 1, :]).T
            row_part = jnp.where(ci <= ri, lgr[k:k + 1, :], lgr[hp + k:hp + k + 1, :])
            e = jnp.where(ci == ri, dsum[k:k + 1, :], jnp.exp2(col_part + row_part))
            ms.append((g * e).astype(BF16))
        mcat = jnp.concatenate(ms, axis=1)
        xc = x_ref[d["rows"], :]
        xstack = jnp.concatenate(
            [jnp.where(lanehead == k, xc, jnp.zeros_like(xc)) for k in range(hp)], axis=0)
        y = jnp.dot(mcat, xstack, preferred_element_type=F32)
        y = y + d["yoff_f"] + chb * d["scale_b"] + d_ref[0] * xc.astype(F32)
        y_ref[d["rows"], :] = y.astype(y_ref.dtype)
        return hb

    def bwd_trip(t, carry):
        chunks = [nc - 1 - (t * u + j) for j in range(u)]
        st = [None] * u
        st[0] = stage_a(chunks[0])
        if u > 1:
            st[1] = stage_a(chunks[1])
        stage_b(st[0])
        hb = h_ref[...]
        for j in range(u):
            if j + 2 < u:
                st[j + 2] = stage_a(chunks[j + 2])
            if j + 1 < u:
                stage_b(st[j + 1])
            hb = stage_c(st[j], hb)
            st[j] = None
        h_ref[...] = hb
        return carry

    lax.fori_loop(0, nc // u, bwd_trip, 0)


def _ssd_scan(xbc, dt_t, ac_t, tot_t, lg_t, d_lanes):
    l = xbc.shape[0]
    nsteps = SSM_HEADS // HEADS_PER_STEP
    per_group = SSM_HEADS // SSM_GROUPS // HEADS_PER_STEP
    b_off = SSM_D_INNER // SSM_STATE
    c_off = b_off + SSM_GROUPS
    small = pl.BlockSpec((1, 2 * HEADS_PER_STEP, l), lambda j: (j, 0, 0))
    return pl.pallas_call(
        _ssd_kernel,
        grid=(nsteps,),
        in_specs=[pl.BlockSpec((l, SSD_LANES), lambda j: (0, j)),
                  pl.BlockSpec((l, SSM_STATE), lambda j: (0, b_off + j // per_group)),
                  pl.BlockSpec((l, SSM_STATE), lambda j: (0, c_off + j // per_group)),
                  small, small, small, small,
                  pl.BlockSpec((1, 1, SSD_LANES), lambda j: (j, 0, 0))],
        out_specs=pl.BlockSpec((l, SSD_LANES), lambda j: (0, j)),
        out_shape=jax.ShapeDtypeStruct((l, SSM_D_INNER), BF16),
        scratch_shapes=[pltpu.VMEM((l // SSM_CHUNK, SSM_STATE, SSD_LANES), BF16),
                        pltpu.VMEM((SSM_STATE, SSD_LANES), F32)],
        compiler_params=_cparams(1),
        name="ssd_scan",
    )(xbc, xbc, xbc, dt_t, ac_t, tot_t, lg_t, d_lanes)


def _na_window(case, qr):
    if case == 0:
        return 0, NA_WIN_H, -qr
    if case == 1:
        return qr, qr + NA_WIN_H, -(NA_WIN_H // 2) - qr
    return NA_Q_ROWS, NA_BAND_ROWS, -NA_WIN_H - qr


def _na_kernel(rpb_ref, q_ref, k_ref, v_ref, z_ref, o_ref, t_ref, bias_ref, *, rows):
    h = pl.program_id(0)
    nblk = rows // NA_Q_ROWS
    gw = GRID_W
    tq = NA_Q_ROWS * gw
    band = NA_BAND_ROWS * gw
    n_dr = 2 * NA_WIN_H - 1
    n_dc = 2 * NA_WIN_W - 1

    kc = lax.broadcasted_iota(jnp.int32, (gw, 2 * gw), 0)
    lane = lax.broadcasted_iota(jnp.int32, (gw, 2 * gw), 1)
    qc = lane % gw
    diff = kc - qc + (NA_WIN_W - 1)
    cs = jnp.clip(qc - NA_WIN_W // 2, 0, gw - NA_WIN_W)
    col_ok = (kc >= cs) & (kc < cs + NA_WIN_W)
    for dr in range(n_dr):
        t = jnp.zeros((gw, 2 * gw), F32)
        for j in range(n_dc):
            t = jnp.where(diff == j, rpb_ref[(h * n_dr + dr) * n_dc + j] * LOG2E, t)
        t_ref[dr] = jnp.where(col_ok, t, NEG_BIG)
    neg = jnp.full((gw, 2 * gw), NEG_BIG, F32)
    for case in range(3):
        for kr in range(NA_BAND_ROWS):
            for pair in range(NA_Q_ROWS // 2):
                halves = []
                for qr in (2 * pair, 2 * pair + 1):
                    lo, hi, off = _na_window(case, qr)
                    halves.append(t_ref[kr + off + NA_WIN_H - 1] if lo <= kr < hi else neg)
                bias_ref[case, kr * gw:(kr + 1) * gw, pair * 2 * gw:(pair + 1) * 2 * gw] = (
                    jnp.where(lane < gw, halves[0], halves[1]))

    def scores(blk):
        _, q0, k0 = blk
        return lax.dot_general(k_ref[pl.ds(k0, band), :], q_ref[pl.ds(q0, tq), :],
                               (((1,), (1,)), ((), ())), preferred_element_type=F32)

    def finish(blk, s):
        case, q0, k0 = blk
        vb = v_ref[pl.ds(k0, band), :]
        ps, rden = [], []
        for pair in range(NA_Q_ROWS // 2):
            wins = [_na_window(case, qr) for qr in (2 * pair, 2 * pair + 1)]
            r0, r1 = min(w[0] for w in wins) * gw, max(w[1] for w in wins) * gw
            lanes = slice(pair * 2 * gw, (pair + 1) * 2 * gw)
            sq = s[r0:r1, lanes] + bias_ref[case, r0:r1, lanes]
            m = jnp.max(sq, axis=0, keepdims=True)
            p = jnp.exp2(sq - m)
            rden.append(1.0 / jnp.sum(p, axis=0, keepdims=True))
            parts = [p.astype(BF16)]
            if r0 > 0:
                parts.insert(0, jnp.zeros((r0, 2 * gw), BF16))
            if r1 < band:
                parts.append(jnp.zeros((band - r1, 2 * gw), BF16))
            ps.append(jnp.concatenate(parts, axis=0))
        o_t = lax.dot_general(vb, jnp.concatenate(ps, axis=1), (((0,), (0,)), ((), ())),
                              preferred_element_type=F32)
        o = (o_t * jnp.concatenate(rden, axis=1)).T
        z = z_ref[pl.ds(q0, tq), :].astype(F32)
        o_ref[pl.ds(q0, tq), :] = (o * (z * jax.nn.sigmoid(z))).astype(o_ref.dtype)

    def run_blocks(blocks):
        s = scores(blocks[0])
        for j, blk in enumerate(blocks):
            s_next = scores(blocks[j + 1]) if j + 1 < len(blocks) else None
            finish(blk, s)
            s = s_next

    def interior(i):
        q0 = i * tq
        if not isinstance(i, int):
            q0 = pl.multiple_of(q0, tq)
        return 1, q0, q0 - (NA_WIN_H // 2) * gw

    u = NA_BLOCKS_PER_TRIP
    top = (0, 0, 0)
    bottom = (2, (nblk - 1) * tq, (rows - NA_BAND_ROWS) * gw)
    run_blocks([top] + [interior(i) for i in range(1, u)])

    def trip(t, carry):
        run_blocks([interior(t * u + j) for j in range(u)])
        return carry

    lax.fori_loop(1, nblk // u - 1, trip, 0)
    run_blocks([interior(i) for i in range(nblk - u, nblk - 1)] + [bottom])


def _neighbourhood_attention(qkvz, rpb):
    l = qkvz.shape[0]
    rows = l // GRID_W
    nblk = rows // NA_Q_ROWS
    assert rows % NA_Q_ROWS == 0 and nblk % NA_BLOCKS_PER_TRIP == 0 and nblk >= 2 * NA_BLOCKS_PER_TRIP
    nh = NA_HEADS

    def col(part):
        return pl.BlockSpec((l, NA_HEAD_DIM), lambda h: (0, part * nh + h))

    return pl.pallas_call(
        functools.partial(_na_kernel, rows=rows),
        grid=(nh,),
        in_specs=[pl.BlockSpec(memory_space=pltpu.SMEM), col(0), col(1), col(2), col(3)],
        out_specs=col(0),
        out_shape=jax.ShapeDtypeStruct((l, NA_D_INNER), BF16),
        scratch_shapes=[pltpu.VMEM((2 * NA_WIN_H - 1, GRID_W, 2 * GRID_W), F32),
                        pltpu.VMEM((3, NA_BAND_ROWS * GRID_W, NA_Q_ROWS * GRID_W), F32)],
        compiler_params=_cparams(1),
        name="neighbourhood_attention",
    )(rpb.reshape(-1), qkvz, qkvz, qkvz, qkvz)


def _ssm_layer(h, norm_w, w_in, conv_w, conv_b, dt_bias, a_log, d_skip, gnorm_w, w_out):
    u = _rmsnorm(h, norm_w, BF16)
    n_zx = SSM_D_INNER + SSM_CONV_DIM
    zx = _matmul(u, w_in, n_zx, BF16, tm=1024, tn=1024)

    hp = HEADS_PER_STEP
    heads = jnp.arange(SSM_HEADS).reshape(SSM_HEADS // hp, 1, hp)
    perm = (heads + jnp.array([0, SSM_HEADS]).reshape(1, 2, 1)).reshape(-1)
    w_dt = w_in[:, n_zx:][:, perm]
    bias = dt_bias.reshape(-1)[perm].reshape(1, -1)
    alog = a_log.reshape(-1)[perm].reshape(-1, 1)
    dt_rows = _dt_project(u, w_dt, bias, alog)
    nsteps = SSM_HEADS // hp
    shp = (nsteps, 2 * hp, h.shape[0])

    xbc = _conv_silu(zx, conv_w, conv_b, SSM_D_INNER)
    d_lanes = jnp.repeat(d_skip, SSM_HEAD_DIM).reshape(nsteps, 1, SSD_LANES)
    y = _ssd_scan(xbc, *[r.reshape(shp) for r in dt_rows], d_lanes)
    g = _gated_rmsnorm(y, zx, gnorm_w)
    return _matmul(g, w_out, w_out.shape[1], F32, tm=1024, tn=512, res=h)


def _na_layer(h, norm_w, w_in, rpb, w_out):
    u = _rmsnorm(h, norm_w, BF16)
    n = w_in.shape[1]
    qscale = NA_HEAD_DIM ** -0.5 * LOG2E
    colscale = jnp.where(jnp.arange(n) < NA_D_INNER, qscale, 1.0).astype(F32).reshape(1, n)
    qkvz = _matmul(u, w_in, n, BF16, tm=1024, tn=1024, colscale=colscale)
    o = _neighbourhood_attention(qkvz, rpb)
    return _matmul(o, w_out, w_out.shape[1], F32, tm=1024, tn=512, res=h)


def kernel(x, norm_w, ssm_w_in, ssm_conv_w, ssm_conv_b, ssm_dt_bias, ssm_A_log, ssm_D, ssm_norm_w,
           ssm_w_out, na_w_in, na_rpb, na_w_out, final_norm_w):
    b, l, d = x.shape
    outs = []
    for bi in range(b):
        h = x[bi]
        h = _ssm_layer(h, norm_w[0], ssm_w_in[0], ssm_conv_w[0], ssm_conv_b[0], ssm_dt_bias[0],
                       ssm_A_log[0], ssm_D[0], ssm_norm_w[0], ssm_w_out[0])
        h = _na_layer(h, norm_w[1], na_w_in[0], na_rpb[0], na_w_out[0])
        outs.append(_rmsnorm(h, final_norm_w, x.dtype))
    return jnp.stack(outs, axis=0)
```

```python
import functools

import jax
import jax.numpy as jnp
from jax import lax
from jax.experimental import pallas as pl
from jax.experimental.pallas import tpu as pltpu

F32 = jnp.float32
BF16 = jnp.bfloat16

GRID_W = 64
SSM_HEAD_DIM = 64
SSM_HEADS = 64
SSM_GROUPS = 8
SSM_STATE = 128
SSM_CONV_W = 7
SSM_CHUNK = 128
SSM_D_INNER = SSM_HEADS * SSM_HEAD_DIM
SSM_CONV_DIM = SSM_D_INNER + 2 * SSM_GROUPS * SSM_STATE
NA_HEAD_DIM = 128
NA_HEADS = 32
NA_D_INNER = NA_HEADS * NA_HEAD_DIM
NA_WIN_H = 8
NA_WIN_W = 16
NORM_EPS = 1e-5

HEADS_PER_STEP = 4
SSD_LANES = HEADS_PER_STEP * SSM_HEAD_DIM
SSD_CHUNKS_PER_TRIP = 8
NA_Q_ROWS = 4
NA_BAND_ROWS = NA_Q_ROWS + NA_WIN_H
NA_BLOCKS_PER_TRIP = 8
LOG2E = 1.4426950408889634
NEG_BIG = -1e30
VMEM_LIMIT = 56 * 1024 * 1024


def _cparams(n_axes):
    return pltpu.CompilerParams(dimension_semantics=("arbitrary",) * n_axes,
                                vmem_limit_bytes=VMEM_LIMIT)


def _rmsnorm_kernel(x_ref, w_ref, o_ref):
    x = x_ref[...]
    ms = jnp.mean(x * x, axis=-1, keepdims=True)
    o_ref[...] = (x * lax.rsqrt(ms + NORM_EPS) * w_ref[...]).astype(o_ref.dtype)


def _rmsnorm(x, w, out_dtype, tm=512):
    l, d = x.shape
    return pl.pallas_call(
        _rmsnorm_kernel,
        grid=(l // tm,),
        in_specs=[pl.BlockSpec((tm, d), lambda i: (i, 0)),
                  pl.BlockSpec((1, d), lambda i: (0, 0))],
        out_specs=pl.BlockSpec((tm, d), lambda i: (i, 0)),
        out_shape=jax.ShapeDtypeStruct((l, d), out_dtype),
        compiler_params=_cparams(1),
        name="rmsnorm",
    )(x, w.reshape(1, d))


def _gated_rmsnorm_kernel(y_ref, z_ref, w_ref, o_ref):
    z = z_ref[...].astype(F32)
    g = y_ref[...].astype(F32) * (z * jax.nn.sigmoid(z))
    ms = jnp.mean(g * g, axis=-1, keepdims=True)
    o_ref[...] = (g * lax.rsqrt(ms + NORM_EPS) * w_ref[...]).astype(o_ref.dtype)


def _gated_rmsnorm(y, zx, w, tm=256):
    l, d = y.shape
    return pl.pallas_call(
        _gated_rmsnorm_kernel,
        grid=(l // tm,),
        in_specs=[pl.BlockSpec((tm, d), lambda i: (i, 0)),
                  pl.BlockSpec((tm, d), lambda i: (i, 0)),
                  pl.BlockSpec((1, d), lambda i: (0, 0))],
        out_specs=pl.BlockSpec((tm, d), lambda i: (i, 0)),
        out_shape=jax.ShapeDtypeStruct((l, d), BF16),
        compiler_params=_cparams(1),
        name="gated_rmsnorm",
    )(y, zx, w.reshape(1, d))


def _matmul_kernel(*refs, has_scale, has_res):
    x_ref, w_ref = refs[0], refs[1]
    pos = 2
    s_ref = r_ref = None
    if has_scale:
        s_ref = refs[pos]
        pos += 1
    if has_res:
        r_ref = refs[pos]
        pos += 1
    o_ref, wbf_ref = refs[pos], refs[pos + 1]

    @pl.when(pl.program_id(1) == 0)
    def _():
        wbf_ref[...] = w_ref[...].astype(BF16)

    acc = jnp.dot(x_ref[...], wbf_ref[...], preferred_element_type=F32)
    if has_scale:
        acc = acc * s_ref[...]
    if has_res:
        acc = acc + r_ref[...]
    o_ref[...] = acc.astype(o_ref.dtype)


def _matmul(x, w, n_out, out_dtype, tm, tn, colscale=None, res=None, single_buffer_w=False):
    l, k = x.shape
    tm = min(tm, l)
    assert l % tm == 0 and n_out % tn == 0
    w_mode = dict(pipeline_mode=pl.Buffered(1)) if single_buffer_w else {}
    in_specs = [pl.BlockSpec((tm, k), lambda n, m: (m, 0)),
                pl.BlockSpec((k, tn), lambda n, m: (0, n), **w_mode)]
    args = [x, w]
    if colscale is not None:
        in_specs.append(pl.BlockSpec((1, tn), lambda n, m: (0, n)))
        args.append(colscale)
    if res is not None:
        in_specs.append(pl.BlockSpec((tm, tn), lambda n, m: (m, n)))
        args.append(res)
    return pl.pallas_call(
        functools.partial(_matmul_kernel, has_scale=colscale is not None, has_res=res is not None),
        grid=(n_out // tn, l // tm),
        in_specs=in_specs,
        out_specs=pl.BlockSpec((tm, tn), lambda n, m: (m, n)),
        out_shape=jax.ShapeDtypeStruct((l, n_out), out_dtype),
        scratch_shapes=[pltpu.VMEM((k, tn), BF16)],
        compiler_params=_cparams(2),
        name="matmul",
    )(*args)


def _split3_dot(a, m_bf16):
    hi = a.astype(BF16)
    r1 = a - hi.astype(F32)
    mid = r1.astype(BF16)
    lo = (r1 - mid.astype(F32)).astype(BF16)
    out = jnp.dot(hi, m_bf16, preferred_element_type=F32)
    out = out + jnp.dot(mid, m_bf16, preferred_element_type=F32)
    return out + jnp.dot(lo, m_bf16, preferred_element_type=F32)


def _dt_kernel(u_ref, w_ref, bias_ref, alog_ref, dt_ref, ac_ref, tot_ref, lg_ref):
    tm = u_ref.shape[0]
    raw = jnp.dot(u_ref[...], w_ref[...].astype(BF16), preferred_element_type=F32)
    xb = raw + bias_ref[...]
    dt = jnp.maximum(xb, 0.0) + jnp.log1p(jnp.exp(-jnp.abs(xb)))
    dt_t = dt.T
    a = dt_t * (-jnp.exp(alog_ref[...]) * LOG2E)
    q = SSM_CHUNK
    si = lax.broadcasted_iota(jnp.int32, (q, q), 0)
    ti = lax.broadcasted_iota(jnp.int32, (q, q), 1)
    upper = jnp.where(si <= ti, 1.0, 0.0).astype(BF16)
    row = lax.broadcasted_iota(jnp.int32, (a.shape[0], q), 0)
    is_bwd = (row % (2 * HEADS_PER_STEP)) >= HEADS_PER_STEP
    dt_ref[...] = dt_t
    log2_dt = jnp.log2(dt_t)
    for c in range(tm // q):
        cols = slice(c * q, (c + 1) * q)
        a_c = a[:, cols]
        inc = _split3_dot(a_c, upper)
        tot_ref[:, cols] = inc
        ac = jnp.where(is_bwd, inc - a_c, inc)
        ac_ref[:, cols] = ac
        lg_ref[:, cols] = log2_dt[:, cols] + jnp.where(is_bwd, ac, -ac)


def _dt_project(u, w_dt, bias, alog, tm=512):
    l, k = u.shape
    n = w_dt.shape[1]
    out = jax.ShapeDtypeStruct((n, l), F32)
    return pl.pallas_call(
        _dt_kernel,
        grid=(l // tm,),
        in_specs=[pl.BlockSpec((tm, k), lambda i: (i, 0)),
                  pl.BlockSpec((k, n), lambda i: (0, 0)),
                  pl.BlockSpec((1, n), lambda i: (0, 0)),
                  pl.BlockSpec((n, 1), lambda i: (0, 0))],
        out_specs=[pl.BlockSpec((n, tm), lambda i: (0, i))] * 4,
        out_shape=[out] * 4,
        compiler_params=_cparams(1),
        name="dt_project",
    )(u, w_dt, bias, alog)


def _conv_silu_kernel(x_ref, w_ref, b_ref, o_ref, *, tile):
    l, cw = x_ref.shape
    pad = 16
    half = SSM_CONV_W // 2
    n_tiles = l // tile
    ext = tile + 2 * pad
    w = w_ref[...]
    b = b_ref[...]
    taps = [j for j in range(SSM_CONV_W) if j != half]
    ti = lax.broadcasted_iota(jnp.int32, (len(taps) * tile, ext), 0)
    ri = lax.broadcasted_iota(jnp.int32, (len(taps) * tile, ext), 1)
    tap_of_row = jnp.zeros_like(ti)
    for n, j in enumerate(taps):
        tap_of_row = jnp.where(ti // tile == n, j, tap_of_row)
    shift = jnp.where(ri == ti % tile + pad - half + tap_of_row, 1.0, 0.0).astype(BF16)

    def shifted_rows(i):
        r0 = pl.multiple_of(i * tile, tile)
        p0 = pl.multiple_of(jnp.maximum(r0 - pad, 0), pad)
        n0 = pl.multiple_of(jnp.minimum(r0 + tile, l - pad), pad)
        zero = jnp.zeros((pad, cw), BF16)
        prev = jnp.where(i > 0, x_ref[pl.ds(p0, pad), :], zero)
        nxt = jnp.where(i < n_tiles - 1, x_ref[pl.ds(n0, pad), :], zero)
        xe = jnp.concatenate([prev, x_ref[pl.ds(r0, tile), :], nxt], axis=0)
        return jnp.dot(shift, xe, preferred_element_type=F32)

    def body(i, carry):
        shifted = shifted_rows(i)
        r0 = pl.multiple_of(i * tile, tile)
        acc = b + x_ref[pl.ds(r0, tile), :].astype(F32) * w[half:half + 1, :]
        for n, j in enumerate(taps):
            acc = acc + shifted[n * tile:(n + 1) * tile, :] * w[j:j + 1, :]
        o_ref[pl.ds(r0, tile), :] = (acc * jax.nn.sigmoid(acc)).astype(o_ref.dtype)
        return carry

    lax.fori_loop(0, n_tiles, body, 0, unroll=min(16, n_tiles))


def _conv_silu(zx, conv_w, conv_b, col0, cw=256, tile=128):
    l = zx.shape[0]
    c = conv_w.shape[1]
    off = col0 // cw
    return pl.pallas_call(
        functools.partial(_conv_silu_kernel, tile=tile),
        grid=(c // cw,),
        in_specs=[pl.BlockSpec((l, cw), lambda j: (0, off + j)),
                  pl.BlockSpec((SSM_CONV_W, cw), lambda j: (0, j)),
                  pl.BlockSpec((1, cw), lambda j: (0, j))],
        out_specs=pl.BlockSpec((l, cw), lambda j: (0, j)),
        out_shape=jax.ShapeDtypeStruct((l, c), BF16),
        compiler_params=_cparams(1),
        name="conv_silu",
    )(zx, conv_w, conv_b.reshape(1, c))


def _head_selector(first_rows):
    n = len(first_rows)
    shape = (3 * 2 * HEADS_PER_STEP * n, SSD_LANES * n)
    row = lax.broadcasted_iota(jnp.int32, shape, 0)
    lane = lax.broadcasted_iota(jnp.int32, shape, 1)
    first = jnp.zeros(shape, jnp.int32)
    for i, f in enumerate(first_rows):
        first = jnp.where(lane // SSD_LANES == i, f, first)
    hit = (row // (3 * 2 * HEADS_PER_STEP) == lane // SSD_LANES) & (
        row % (2 * HEADS_PER_STEP) == (lane % SSD_LANES) // SSM_HEAD_DIM + first)
    return jnp.where(hit, 1.0, 0.0).astype(BF16)


def _rows_to_lanes(blocks, selector):
    pieces = []
    for r8 in blocks:
        hi = r8.astype(BF16).astype(F32)
        r1 = r8 - hi
        mid = r1.astype(BF16).astype(F32)
        pieces += [hi, mid, r1 - mid]
    lhs = jnp.concatenate(pieces, axis=0).astype(BF16)
    return lax.dot_general(lhs, selector, (((0,), (0,)), ((), ())), preferred_element_type=F32)


def _head_lane_vector(c4, lanehead):
    out = jnp.broadcast_to(c4[0:1, :], (1, SSD_LANES))
    for k in range(1, HEADS_PER_STEP):
        out = jnp.where(lanehead == k, jnp.broadcast_to(c4[k:k + 1, :], (1, SSD_LANES)), out)
    return out


def _ssd_kernel(x_ref, b_ref, c_ref, dt_ref, ac_ref, tot_ref, lg_ref, d_ref, y_ref, hf_ref, h_ref):
    l = x_ref.shape[0]
    q = SSM_CHUNK
    nc = l // q
    hp = HEADS_PER_STEP
    lanehead = lax.broadcasted_iota(jnp.int32, (1, SSD_LANES), 1) // SSM_HEAD_DIM
    ri = lax.broadcasted_iota(jnp.int32, (q, q), 0)
    ci = lax.broadcasted_iota(jnp.int32, (q, q), 1)
    contract0 = (((0,), (0,)), ((), ()))
    contract_last = (((1,), (1,)), ((), ()))
    sel_f = _head_selector([0])
    sel_fbb = _head_selector([0, hp, hp])
    u = SSD_CHUNKS_PER_TRIP

    def chunk_rows(c):
        return pl.ds(pl.multiple_of(c * q, q), q)


    h_ref[...] = jnp.zeros_like(h_ref)

    def fwd_trip(t, carry):
        chunks = [t * u + j for j in range(u)]
        a_lasts, wfs, ss = [], [], []
        for c in chunks:
            rows = chunk_rows(c)
            acr = ac_ref[0, :, rows]
            a_lasts.append(acr[:, q - 1:q])
            wfs.append(_rows_to_lanes([jnp.exp2(a_lasts[-1] - acr) * dt_ref[0, :, rows]], sel_f))
        for c, wf in zip(chunks, wfs):
            rows = chunk_rows(c)
            xw = (x_ref[rows, :].astype(F32) * wf).astype(BF16)
            ss.append(lax.dot_general(b_ref[rows, :], xw, contract0, preferred_element_type=F32))
        h = h_ref[...]
        for c, a_last, s in zip(chunks, a_lasts, ss):
            hf_ref[c] = h.astype(BF16)
            h = h * _head_lane_vector(jnp.exp2(a_last[0:hp]), lanehead) + s
        h_ref[...] = h
        return carry

    lax.fori_loop(0, nc // u, fwd_trip, 0)

    h_ref[...] = jnp.zeros_like(h_ref)

    def stage_a(c):
        rows = chunk_rows(c)
        bc, cc = b_ref[rows, :], c_ref[rows, :]
        dtr, acr = dt_ref[0, :, rows], ac_ref[0, :, rows]
        t_last = tot_ref[0, :, rows][:, q - 1:q]
        g = lax.dot_general(cc, bc, contract_last, preferred_element_type=F32)
        scales = _rows_to_lanes([jnp.exp2(acr), jnp.exp2(t_last - acr), jnp.exp2(acr) * dtr], sel_fbb)
        yoff_f = jnp.dot(cc, hf_ref[c], preferred_element_type=F32) * scales[:, 0:SSD_LANES]
        return dict(rows=rows, bc=bc, cc=cc, dtr=dtr, acr=acr, t_last=t_last, g=g, yoff_f=yoff_f,
                    scale_b=scales[:, SSD_LANES:2 * SSD_LANES], wb=scales[:, 2 * SSD_LANES:])

    def stage_b(d):
        xw = (x_ref[d["rows"], :].astype(F32) * d["wb"]).astype(BF16)
        d["s"] = lax.dot_general(d["bc"], xw, contract0, preferred_element_type=F32)

    def stage_c(d, hb):
        chb = jnp.dot(d["cc"], hb.astype(BF16), preferred_element_type=F32)
        hb = hb * _head_lane_vector(jnp.exp2(d["t_last"][hp:2 * hp]), lanehead) + d["s"]
        dtr, acr, g = d["dtr"], d["acr"], d["g"]
        lgr = lg_ref[0, :, d["rows"]]
        ncb = -acr[hp:2 * hp]
        dsum = dtr[0:hp] + dtr[hp:2 * hp]
        ms = []
        for k in range(hp):
            col_part = jnp.where(ri <= ci, acr[k:k + 1, :], ncb[k:k + 1, :]).T
            row_part = jnp.where(ci <= ri, lgr[k:k + 1, :], lgr[hp + k:hp + k + 1, :])
            e = jnp.where(ci == ri, dsum[k:k + 1, :], jnp.exp2(col_part + row_part))
            ms.append((g * e).astype(BF16))
        mcat = jnp.concatenate(ms, axis=1)
        xc = x_ref[d["rows"], :]
        xstack = jnp.concatenate(
            [jnp.where(lanehead == k, xc, jnp.zeros_like(xc)) for k in range(hp)], axis=0)
        y = jnp.dot(mcat, xstack, preferred_element_type=F32)
        y = y + d["yoff_f"] + chb * d["scale_b"] + d_ref[0] * xc.astype(F32)
        y_ref[d["rows"], :] = y.astype(y_ref.dtype)
        return hb

    def bwd_trip(t, carry):
        chunks = [nc - 1 - (t * u + j) for j in range(u)]
        st = [None] * u
        st[0] = stage_a(chunks[0])
        if u > 1:
            st[1] = stage_a(chunks[1])
        stage_b(st[0])
        hb = h_ref[...]
        for j in range(u):
            if j + 2 < u:
                st[j + 2] = stage_a(chunks[j + 2])
            if j + 1 < u:
                stage_b(st[j + 1])
            hb = stage_c(st[j], hb)
            st[j] = None
        h_ref[...] = hb
        return carry

    lax.fori_loop(0, nc // u, bwd_trip, 0)


def _ssd_scan(xbc, dt_t, ac_t, tot_t, lg_t, d_lanes):
    l = xbc.shape[0]
    assert l % (SSM_CHUNK * SSD_CHUNKS_PER_TRIP) == 0
    nsteps = SSM_HEADS // HEADS_PER_STEP
    per_group = SSM_HEADS // SSM_GROUPS // HEADS_PER_STEP
    b_off = SSM_D_INNER // SSM_STATE
    c_off = b_off + SSM_GROUPS
    small = pl.BlockSpec((1, 2 * HEADS_PER_STEP, l), lambda j: (j, 0, 0))
    return pl.pallas_call(
        _ssd_kernel,
        grid=(nsteps,),
        in_specs=[pl.BlockSpec((l, SSD_LANES), lambda j: (0, j)),
                  pl.BlockSpec((l, SSM_STATE), lambda j: (0, b_off + j // per_group)),
                  pl.BlockSpec((l, SSM_STATE), lambda j: (0, c_off + j // per_group)),
                  small, small, small, small,
                  pl.BlockSpec((1, 1, SSD_LANES), lambda j: (j, 0, 0))],
        out_specs=pl.BlockSpec((l, SSD_LANES), lambda j: (0, j)),
        out_shape=jax.ShapeDtypeStruct((l, SSM_D_INNER), BF16),
        scratch_shapes=[pltpu.VMEM((l // SSM_CHUNK, SSM_STATE, SSD_LANES), BF16),
                        pltpu.VMEM((SSM_STATE, SSD_LANES), F32)],
        compiler_params=_cparams(1),
        name="ssd_scan",
    )(xbc, xbc, xbc, dt_t, ac_t, tot_t, lg_t, d_lanes)


def _na_window(case, qr):
    if case == 0:
        return 0, NA_WIN_H, -qr
    if case == 1:
        return qr, qr + NA_WIN_H, -(NA_WIN_H // 2) - qr
    return NA_Q_ROWS, NA_BAND_ROWS, -NA_WIN_H - qr


def _na_kernel(rpb_ref, q_ref, k_ref, v_ref, z_ref, o_ref, t_ref, bias_ref, *, rows):
    h = pl.program_id(0)
    nblk = rows // NA_Q_ROWS
    gw = GRID_W
    tq = NA_Q_ROWS * gw
    band = NA_BAND_ROWS * gw
    n_dr = 2 * NA_WIN_H - 1
    n_dc = 2 * NA_WIN_W - 1

    kc = lax.broadcasted_iota(jnp.int32, (gw, 2 * gw), 0)
    lane = lax.broadcasted_iota(jnp.int32, (gw, 2 * gw), 1)
    qc = lane % gw
    diff = kc - qc + (NA_WIN_W - 1)
    cs = jnp.clip(qc - NA_WIN_W // 2, 0, gw - NA_WIN_W)
    col_ok = (kc >= cs) & (kc < cs + NA_WIN_W)
    for dr in range(n_dr):
        t = jnp.zeros((gw, 2 * gw), F32)
        for j in range(n_dc):
            t = jnp.where(diff == j, rpb_ref[(h * n_dr + dr) * n_dc + j] * LOG2E, t)
        t_ref[dr] = jnp.where(col_ok, t, NEG_BIG)
    neg = jnp.full((gw, 2 * gw), NEG_BIG, F32)
    for case in range(3):
        for kr in range(NA_BAND_ROWS):
            for pair in range(NA_Q_ROWS // 2):
                halves = []
                for qr in (2 * pair, 2 * pair + 1):
                    lo, hi, off = _na_window(case, qr)
                    halves.append(t_ref[kr + off + NA_WIN_H - 1] if lo <= kr < hi else neg)
                bias_ref[case, kr * gw:(kr + 1) * gw, pair * 2 * gw:(pair + 1) * 2 * gw] = (
                    jnp.where(lane < gw, halves[0], halves[1]))

    def scores(blk):
        _, q0, k0 = blk
        return lax.dot_general(k_ref[pl.ds(k0, band), :], q_ref[pl.ds(q0, tq), :],
                               (((1,), (1,)), ((), ())), preferred_element_type=F32)

    def finish(blk, s):
        case, q0, k0 = blk
        vb = v_ref[pl.ds(k0, band), :]
        ps, rden = [], []
        for pair in range(NA_Q_ROWS // 2):
            wins = [_na_window(case, qr) for qr in (2 * pair, 2 * pair + 1)]
            r0, r1 = min(w[0] for w in wins) * gw, max(w[1] for w in wins) * gw
            lanes = slice(pair * 2 * gw, (pair + 1) * 2 * gw)
            sq = s[r0:r1, lanes] + bias_ref[case, r0:r1, lanes]
            m = jnp.max(sq, axis=0, keepdims=True)
            p = jnp.exp2(sq - m)
            rden.append(1.0 / jnp.sum(p, axis=0, keepdims=True))
            parts = [p.astype(BF16)]
            if r0 > 0:
                parts.insert(0, jnp.zeros((r0, 2 * gw), BF16))
            if r1 < band:
                parts.append(jnp.zeros((band - r1, 2 * gw), BF16))
            ps.append(jnp.concatenate(parts, axis=0))
        o_t = lax.dot_general(vb, jnp.concatenate(ps, axis=1), (((0,), (0,)), ((), ())),
                              preferred_element_type=F32)
        o = (o_t * jnp.concatenate(rden, axis=1)).T
        z = z_ref[pl.ds(q0, tq), :].astype(F32)
        o_ref[pl.ds(q0, tq), :] = (o * (z * jax.nn.sigmoid(z))).astype(o_ref.dtype)

    def run_blocks(blocks):
        s = scores(blocks[0])
        for j, blk in enumerate(blocks):
            s_next = scores(blocks[j + 1]) if j + 1 < len(blocks) else None
            finish(blk, s)
            s = s_next

    def interior(i):
        q0 = i * tq
        if not isinstance(i, int):
            q0 = pl.multiple_of(q0, tq)
        return 1, q0, q0 - (NA_WIN_H // 2) * gw

    u = NA_BLOCKS_PER_TRIP
    top = (0, 0, 0)
    bottom = (2, (nblk - 1) * tq, (rows - NA_BAND_ROWS) * gw)
    run_blocks([top] + [interior(i) for i in range(1, u)])

    def trip(t, carry):
        run_blocks([interior(t * u + j) for j in range(u)])
        return carry

    lax.fori_loop(1, nblk // u - 1, trip, 0)
    run_blocks([interior(i) for i in range(nblk - u, nblk - 1)] + [bottom])


def _neighbourhood_attention(qkvz, rpb):
    l = qkvz.shape[0]
    rows = l // GRID_W
    nblk = rows // NA_Q_ROWS
    assert rows % NA_Q_ROWS == 0 and nblk % NA_BLOCKS_PER_TRIP == 0 and nblk >= 2 * NA_BLOCKS_PER_TRIP
    nh = NA_HEADS

    def col(part):
        return pl.BlockSpec((l, NA_HEAD_DIM), lambda h: (0, part * nh + h))

    return pl.pallas_call(
        functools.partial(_na_kernel, rows=rows),
        grid=(nh,),
        in_specs=[pl.BlockSpec(memory_space=pltpu.SMEM), col(0), col(1), col(2), col(3)],
        out_specs=col(0),
        out_shape=jax.ShapeDtypeStruct((l, NA_D_INNER), BF16),
        scratch_shapes=[pltpu.VMEM((2 * NA_WIN_H - 1, GRID_W, 2 * GRID_W), F32),
                        pltpu.VMEM((3, NA_BAND_ROWS * GRID_W, NA_Q_ROWS * GRID_W), F32)],
        compiler_params=_cparams(1),
        name="neighbourhood_attention",
    )(rpb.reshape(-1), qkvz, qkvz, qkvz, qkvz)


def _ssm_layer(h, norm_w, w_in, conv_w, conv_b, dt_bias, a_log, d_skip, gnorm_w, w_out):
    u = _rmsnorm(h, norm_w, BF16)
    n_zx = SSM_D_INNER + SSM_CONV_DIM
    zx = _matmul(u, w_in, n_zx, BF16, tm=1024, tn=1024)

    hp = HEADS_PER_STEP
    heads = jnp.arange(SSM_HEADS).reshape(SSM_HEADS // hp, 1, hp)
    perm = (heads + jnp.array([0, SSM_HEADS]).reshape(1, 2, 1)).reshape(-1)
    w_dt = w_in[:, n_zx:][:, perm]
    bias = dt_bias.reshape(-1)[perm].reshape(1, -1)
    alog = a_log.reshape(-1)[perm].reshape(-1, 1)
    dt_rows = _dt_project(u, w_dt, bias, alog)
    nsteps = SSM_HEADS // hp
    shp = (nsteps, 2 * hp, h.shape[0])

    xbc = _conv_silu(zx, conv_w, conv_b, SSM_D_INNER)
    d_lanes = jnp.repeat(d_skip, SSM_HEAD_DIM).reshape(nsteps, 1, SSD_LANES)
    y = _ssd_scan(xbc, *[r.reshape(shp) for r in dt_rows], d_lanes)
    g = _gated_rmsnorm(y, zx, gnorm_w)
    return _matmul(g, w_out, w_out.shape[1], F32, tm=512, tn=1024, res=h, single_buffer_w=True)


def _na_layer(h, norm_w, w_in, rpb, w_out):
    u = _rmsnorm(h, norm_w, BF16)
    n = w_in.shape[1]
    qscale = NA_HEAD_DIM ** -0.5 * LOG2E
    colscale = jnp.where(jnp.arange(n) < NA_D_INNER, qscale, 1.0).astype(F32).reshape(1, n)
    qkvz = _matmul(u, w_in, n, BF16, tm=1024, tn=1024, colscale=colscale)
    o = _neighbourhood_attention(qkvz, rpb)
    return _matmul(o, w_out, w_out.shape[1], F32, tm=512, tn=1024, res=h, single_buffer_w=True)


def kernel(x, norm_w, ssm_w_in, ssm_conv_w, ssm_conv_b, ssm_dt_bias, ssm_A_log, ssm_D, ssm_norm_w,
           ssm_w_out, na_w_in, na_rpb, na_w_out, final_norm_w):
    b, l, d = x.shape
    outs = []
    for bi in range(b):
        h = x[bi]
        h = _ssm_layer(h, norm_w[0], ssm_w_in[0], ssm_conv_w[0], ssm_conv_b[0], ssm_dt_bias[0],
                       ssm_A_log[0], ssm_D[0], ssm_norm_w[0], ssm_w_out[0])
        h = _na_layer(h, norm_w[1], na_w_in[0], na_rpb[0], na_w_out[0])
        outs.append(_rmsnorm(h, final_norm_w, x.dtype))
    return jnp.stack(outs, axis=0)
```

```python
import functools

import jax
import jax.numpy as jnp
from jax import lax
from jax.experimental import pallas as pl
from jax.experimental.pallas import tpu as pltpu

F32 = jnp.float32
BF16 = jnp.bfloat16

GRID_W = 64
SSM_HEAD_DIM = 64
SSM_HEADS = 64
SSM_GROUPS = 8
SSM_STATE = 128
SSM_CONV_W = 7
SSM_CHUNK = 128
SSM_D_INNER = SSM_HEADS * SSM_HEAD_DIM
SSM_CONV_DIM = SSM_D_INNER + 2 * SSM_GROUPS * SSM_STATE
NA_HEAD_DIM = 128
NA_HEADS = 32
NA_D_INNER = NA_HEADS * NA_HEAD_DIM
NA_WIN_H = 8
NA_WIN_W = 16
NORM_EPS = 1e-5

HEADS_PER_STEP = 4
SSD_LANES = HEADS_PER_STEP * SSM_HEAD_DIM
SSD_CHUNKS_PER_TRIP = 8
NA_Q_ROWS = 4
NA_BAND_ROWS = NA_Q_ROWS + NA_WIN_H
NA_BLOCKS_PER_TRIP = 8
LOG2E = 1.4426950408889634
NEG_BIG = -1e30
VMEM_LIMIT = 56 * 1024 * 1024


def _cparams(n_axes):
    return pltpu.CompilerParams(dimension_semantics=("arbitrary",) * n_axes,
                                vmem_limit_bytes=VMEM_LIMIT)


def _rmsnorm_kernel(x_ref, w_ref, o_ref):
    x = x_ref[...]
    ms = jnp.mean(x * x, axis=-1, keepdims=True)
    o_ref[...] = (x * lax.rsqrt(ms + NORM_EPS) * w_ref[...]).astype(o_ref.dtype)


def _rmsnorm(x, w, out_dtype, tm=1024):
    l, d = x.shape
    tm = min(tm, l)
    return pl.pallas_call(
        _rmsnorm_kernel,
        grid=(l // tm,),
        in_specs=[pl.BlockSpec((tm, d), lambda i: (i, 0)),
                  pl.BlockSpec((1, d), lambda i: (0, 0))],
        out_specs=pl.BlockSpec((tm, d), lambda i: (i, 0)),
        out_shape=jax.ShapeDtypeStruct((l, d), out_dtype),
        compiler_params=_cparams(1),
        name="rmsnorm",
    )(x, w.reshape(1, d))


def _gated_rmsnorm_kernel(y_ref, z_ref, w_ref, o_ref):
    z = z_ref[...].astype(F32)
    g = y_ref[...].astype(F32) * (z * jax.nn.sigmoid(z))
    ms = jnp.mean(g * g, axis=-1, keepdims=True)
    o_ref[...] = (g * lax.rsqrt(ms + NORM_EPS) * w_ref[...]).astype(o_ref.dtype)


def _gated_rmsnorm(y, zx, w, tm=512):
    l, d = y.shape
    tm = min(tm, l)
    return pl.pallas_call(
        _gated_rmsnorm_kernel,
        grid=(l // tm,),
        in_specs=[pl.BlockSpec((tm, d), lambda i: (i, 0)),
                  pl.BlockSpec((tm, d), lambda i: (i, 0)),
                  pl.BlockSpec((1, d), lambda i: (0, 0))],
        out_specs=pl.BlockSpec((tm, d), lambda i: (i, 0)),
        out_shape=jax.ShapeDtypeStruct((l, d), BF16),
        compiler_params=_cparams(1),
        name="gated_rmsnorm",
    )(y, zx, w.reshape(1, d))


def _matmul_kernel(*refs, has_scale, has_res):
    x_ref, w_ref = refs[0], refs[1]
    pos = 2
    s_ref = r_ref = None
    if has_scale:
        s_ref = refs[pos]
        pos += 1
    if has_res:
        r_ref = refs[pos]
        pos += 1
    o_ref, wbf_ref = refs[pos], refs[pos + 1]

    @pl.when(pl.program_id(1) == 0)
    def _():
        wbf_ref[...] = w_ref[...].astype(BF16)

    acc = jnp.dot(x_ref[...], wbf_ref[...], preferred_element_type=F32)
    if has_scale:
        acc = acc * s_ref[...]
    if has_res:
        acc = acc + r_ref[...]
    o_ref[...] = acc.astype(o_ref.dtype)


def _matmul(x, w, n_out, out_dtype, tm, tn, colscale=None, res=None, single_buffer_w=False):
    l, k = x.shape
    tm = min(tm, l)
    assert l % tm == 0 and n_out % tn == 0
    w_mode = dict(pipeline_mode=pl.Buffered(1)) if single_buffer_w else {}
    in_specs = [pl.BlockSpec((tm, k), lambda n, m: (m, 0)),
                pl.BlockSpec((k, tn), lambda n, m: (0, n), **w_mode)]
    args = [x, w]
    if colscale is not None:
        in_specs.append(pl.BlockSpec((1, tn), lambda n, m: (0, n)))
        args.append(colscale)
    if res is not None:
        in_specs.append(pl.BlockSpec((tm, tn), lambda n, m: (m, n)))
        args.append(res)
    return pl.pallas_call(
        functools.partial(_matmul_kernel, has_scale=colscale is not None, has_res=res is not None),
        grid=(n_out // tn, l // tm),
        in_specs=in_specs,
        out_specs=pl.BlockSpec((tm, tn), lambda n, m: (m, n)),
        out_shape=jax.ShapeDtypeStruct((l, n_out), out_dtype),
        scratch_shapes=[pltpu.VMEM((k, tn), BF16)],
        compiler_params=_cparams(2),
        name="matmul",
    )(*args)


def _split3_dot(a, m_bf16):
    hi = a.astype(BF16)
    r1 = a - hi.astype(F32)
    mid = r1.astype(BF16)
    lo = (r1 - mid.astype(F32)).astype(BF16)
    out = jnp.dot(hi, m_bf16, preferred_element_type=F32)
    out = out + jnp.dot(mid, m_bf16, preferred_element_type=F32)
    return out + jnp.dot(lo, m_bf16, preferred_element_type=F32)


def _dt_kernel(u_ref, w_ref, bias_ref, alog_ref, dt_ref, ac_ref, tot_ref, lg_ref):
    tm = u_ref.shape[0]
    raw = jnp.dot(u_ref[...], w_ref[...].astype(BF16), preferred_element_type=F32)
    xb = raw + bias_ref[...]
    dt = jnp.maximum(xb, 0.0) + jnp.log1p(jnp.exp(-jnp.abs(xb)))
    dt_t = dt.T
    a = dt_t * (-jnp.exp(alog_ref[...]) * LOG2E)
    q = SSM_CHUNK
    si = lax.broadcasted_iota(jnp.int32, (q, q), 0)
    ti = lax.broadcasted_iota(jnp.int32, (q, q), 1)
    upper = jnp.where(si <= ti, 1.0, 0.0).astype(BF16)
    row = lax.broadcasted_iota(jnp.int32, (a.shape[0], q), 0)
    is_bwd = (row % (2 * HEADS_PER_STEP)) >= HEADS_PER_STEP
    dt_ref[...] = dt_t
    log2_dt = jnp.log2(dt_t)
    for c in range(tm // q):
        cols = slice(c * q, (c + 1) * q)
        a_c = a[:, cols]
        inc = _split3_dot(a_c, upper)
        tot_ref[:, cols] = inc
        ac = jnp.where(is_bwd, inc - a_c, inc)
        ac_ref[:, cols] = ac
        lg_ref[:, cols] = log2_dt[:, cols] + jnp.where(is_bwd, ac, -ac)


def _dt_project(u, w_dt, bias, alog, tm=512):
    l, k = u.shape
    n = w_dt.shape[1]
    out = jax.ShapeDtypeStruct((n, l), F32)
    return pl.pallas_call(
        _dt_kernel,
        grid=(l // tm,),
        in_specs=[pl.BlockSpec((tm, k), lambda i: (i, 0)),
                  pl.BlockSpec((k, n), lambda i: (0, 0)),
                  pl.BlockSpec((1, n), lambda i: (0, 0)),
                  pl.BlockSpec((n, 1), lambda i: (0, 0))],
        out_specs=[pl.BlockSpec((n, tm), lambda i: (0, i))] * 4,
        out_shape=[out] * 4,
        compiler_params=_cparams(1),
        name="dt_project",
    )(u, w_dt, bias, alog)


def _conv_silu_kernel(x_ref, w_ref, b_ref, o_ref, *, tile):
    l, cw = x_ref.shape
    pad = 16
    half = SSM_CONV_W // 2
    n_tiles = l // tile
    ext = tile + 2 * pad
    w = w_ref[...]
    b = b_ref[...]
    taps = [j for j in range(SSM_CONV_W) if j != half]
    ti = lax.broadcasted_iota(jnp.int32, (len(taps) * tile, ext), 0)
    ri = lax.broadcasted_iota(jnp.int32, (len(taps) * tile, ext), 1)
    tap_of_row = jnp.zeros_like(ti)
    for n, j in enumerate(taps):
        tap_of_row = jnp.where(ti // tile == n, j, tap_of_row)
    shift = jnp.where(ri == ti % tile + pad - half + tap_of_row, 1.0, 0.0).astype(BF16)

    def shifted_rows(i):
        r0 = pl.multiple_of(i * tile, tile)
        p0 = pl.multiple_of(jnp.maximum(r0 - pad, 0), pad)
        n0 = pl.multiple_of(jnp.minimum(r0 + tile, l - pad), pad)
        zero = jnp.zeros((pad, cw), BF16)
        prev = jnp.where(i > 0, x_ref[pl.ds(p0, pad), :], zero)
        nxt = jnp.where(i < n_tiles - 1, x_ref[pl.ds(n0, pad), :], zero)
        xe = jnp.concatenate([prev, x_ref[pl.ds(r0, tile), :], nxt], axis=0)
        return jnp.dot(shift, xe, preferred_element_type=F32)

    def body(i, carry):
        shifted = shifted_rows(i)
        r0 = pl.multiple_of(i * tile, tile)
        acc = b + x_ref[pl.ds(r0, tile), :].astype(F32) * w[half:half + 1, :]
        for n, j in enumerate(taps):
            acc = acc + shifted[n * tile:(n + 1) * tile, :] * w[j:j + 1, :]
        o_ref[pl.ds(r0, tile), :] = (acc * jax.nn.sigmoid(acc)).astype(o_ref.dtype)
        return carry

    lax.fori_loop(0, n_tiles, body, 0, unroll=min(16, n_tiles))


def _conv_silu(zx, conv_w, conv_b, col0, cw=256, tile=128):
    l = zx.shape[0]
    c = conv_w.shape[1]
    off = col0 // cw
    return pl.pallas_call(
        functools.partial(_conv_silu_kernel, tile=tile),
        grid=(c // cw,),
        in_specs=[pl.BlockSpec((l, cw), lambda j: (0, off + j)),
                  pl.BlockSpec((SSM_CONV_W, cw), lambda j: (0, j)),
                  pl.BlockSpec((1, cw), lambda j: (0, j))],
        out_specs=pl.BlockSpec((l, cw), lambda j: (0, j)),
        out_shape=jax.ShapeDtypeStruct((l, c), BF16),
        compiler_params=_cparams(1),
        name="conv_silu",
    )(zx, conv_w, conv_b.reshape(1, c))


def _head_selector(first_rows):
    n = len(first_rows)
    shape = (3 * 2 * HEADS_PER_STEP * n, SSD_LANES * n)
    row = lax.broadcasted_iota(jnp.int32, shape, 0)
    lane = lax.broadcasted_iota(jnp.int32, shape, 1)
    first = jnp.zeros(shape, jnp.int32)
    for i, f in enumerate(first_rows):
        first = jnp.where(lane // SSD_LANES == i, f, first)
    hit = (row // (3 * 2 * HEADS_PER_STEP) == lane // SSD_LANES) & (
        row % (2 * HEADS_PER_STEP) == (lane % SSD_LANES) // SSM_HEAD_DIM + first)
    return jnp.where(hit, 1.0, 0.0).astype(BF16)


def _rows_to_lanes(blocks, selector):
    pieces = []
    for r8 in blocks:
        hi = r8.astype(BF16).astype(F32)
        r1 = r8 - hi
        mid = r1.astype(BF16).astype(F32)
        pieces += [hi, mid, r1 - mid]
    lhs = jnp.concatenate(pieces, axis=0).astype(BF16)
    return lax.dot_general(lhs, selector, (((0,), (0,)), ((), ())), preferred_element_type=F32)


def _head_lane_vector(c4, lanehead):
    out = jnp.broadcast_to(c4[0:1, :], (1, SSD_LANES))
    for k in range(1, HEADS_PER_STEP):
        out = jnp.where(lanehead == k, jnp.broadcast_to(c4[k:k + 1, :], (1, SSD_LANES)), out)
    return out


def _ssd_kernel(x_ref, b_ref, c_ref, dt_ref, ac_ref, tot_ref, lg_ref, d_ref, y_ref, hf_ref, h_ref):
    l = x_ref.shape[0]
    q = SSM_CHUNK
    nc = l // q
    hp = HEADS_PER_STEP
    lanehead = lax.broadcasted_iota(jnp.int32, (1, SSD_LANES), 1) // SSM_HEAD_DIM
    ri = lax.broadcasted_iota(jnp.int32, (q, q), 0)
    ci = lax.broadcasted_iota(jnp.int32, (q, q), 1)
    contract0 = (((0,), (0,)), ((), ()))
    contract_last = (((1,), (1,)), ((), ()))
    sel_f = _head_selector([0])
    sel_fbb = _head_selector([0, hp, hp])
    u = SSD_CHUNKS_PER_TRIP

    def chunk_rows(c):
        return pl.ds(pl.multiple_of(c * q, q), q)


    h_ref[...] = jnp.zeros_like(h_ref)

    def fwd_trip(t, carry):
        chunks = [t * u + j for j in range(u)]
        a_lasts, wfs, ss = [], [], []
        for c in chunks:
            rows = chunk_rows(c)
            acr = ac_ref[0, :, rows]
            a_lasts.append(acr[:, q - 1:q])
            wfs.append(_rows_to_lanes([jnp.exp2(a_lasts[-1] - acr) * dt_ref[0, :, rows]], sel_f))
        for c, wf in zip(chunks, wfs):
            rows = chunk_rows(c)
            xw = (x_ref[rows, :].astype(F32) * wf).astype(BF16)
            ss.append(lax.dot_general(b_ref[rows, :], xw, contract0, preferred_element_type=F32))
        h = h_ref[...]
        for c, a_last, s in zip(chunks, a_lasts, ss):
            hf_ref[c] = h.astype(BF16)
            h = h * _head_lane_vector(jnp.exp2(a_last[0:hp]), lanehead) + s
        h_ref[...] = h
        return carry

    lax.fori_loop(0, nc // u, fwd_trip, 0)

    h_ref[...] = jnp.zeros_like(h_ref)

    def stage_a(c):
        rows = chunk_rows(c)
        bc, cc = b_ref[rows, :], c_ref[rows, :]
        dtr, acr = dt_ref[0, :, rows], ac_ref[0, :, rows]
        t_last = tot_ref[0, :, rows][:, q - 1:q]
        g = lax.dot_general(cc, bc, contract_last, preferred_element_type=F32)
        scales = _rows_to_lanes([jnp.exp2(acr), jnp.exp2(t_last - acr), jnp.exp2(acr) * dtr], sel_fbb)
        yoff_f = jnp.dot(cc, hf_ref[c], preferred_element_type=F32) * scales[:, 0:SSD_LANES]
        return dict(rows=rows, bc=bc, cc=cc, dtr=dtr, acr=acr, t_last=t_last, g=g, yoff_f=yoff_f,
                    scale_b=scales[:, SSD_LANES:2 * SSD_LANES], wb=scales[:, 2 * SSD_LANES:])

    def stage_b(d):
        xw = (x_ref[d["rows"], :].astype(F32) * d["wb"]).astype(BF16)
        d["s"] = lax.dot_general(d["bc"], xw, contract0, preferred_element_type=F32)

    def stage_c(d, hb):
        chb = jnp.dot(d["cc"], hb.astype(BF16), preferred_element_type=F32)
        hb = hb * _head_lane_vector(jnp.exp2(d["t_last"][hp:2 * hp]), lanehead) + d["s"]
        dtr, acr, g = d["dtr"], d["acr"], d["g"]
        lgr = lg_ref[0, :, d["rows"]]
        ncb = -acr[hp:2 * hp]
        dsum = dtr[0:hp] + dtr[hp:2 * hp]
        ms = []
        for k in range(hp):
            col_part = jnp.where(ri <= ci, acr[k:k + 1, :], ncb[k:k + 1, :]).T
            row_part = jnp.where(ci <= ri, lgr[k:k + 1, :], lgr[hp + k:hp + k + 1, :])
            e = jnp.where(ci == ri, dsum[k:k + 1, :], jnp.exp2(col_part + row_part))
            ms.append((g * e).astype(BF16))
        mcat = jnp.concatenate(ms, axis=1)
        xc = x_ref[d["rows"], :]
        xstack = jnp.concatenate(
            [jnp.where(lanehead == k, xc, jnp.zeros_like(xc)) for k in range(hp)], axis=0)
        y = jnp.dot(mcat, xstack, preferred_element_type=F32)
        y = y + d["yoff_f"] + chb * d["scale_b"] + d_ref[0] * xc.astype(F32)
        y_ref[d["rows"], :] = y.astype(y_ref.dtype)
        return hb

    def bwd_trip(t, carry):
        chunks = [nc - 1 - (t * u + j) for j in range(u)]
        st = [None] * u
        st[0] = stage_a(chunks[0])
        if u > 1:
            st[1] = stage_a(chunks[1])
        stage_b(st[0])
        hb = h_ref[...]
        for j in range(u):
            if j + 2 < u:
                st[j + 2] = stage_a(chunks[j + 2])
            if j + 1 < u:
                stage_b(st[j + 1])
            hb = stage_c(st[j], hb)
            st[j] = None
        h_ref[...] = hb
        return carry

    lax.fori_loop(0, nc // u, bwd_trip, 0)


def _ssd_scan(xbc, dt_t, ac_t, tot_t, lg_t, d_lanes):
    l = xbc.shape[0]
    assert l % (SSM_CHUNK * SSD_CHUNKS_PER_TRIP) == 0
    nsteps = SSM_HEADS // HEADS_PER_STEP
    per_group = SSM_HEADS // SSM_GROUPS // HEADS_PER_STEP
    b_off = SSM_D_INNER // SSM_STATE
    c_off = b_off + SSM_GROUPS
    small = pl.BlockSpec((1, 2 * HEADS_PER_STEP, l), lambda j: (j, 0, 0))
    return pl.pallas_call(
        _ssd_kernel,
        grid=(nsteps,),
        in_specs=[pl.BlockSpec((l, SSD_LANES), lambda j: (0, j)),
                  pl.BlockSpec((l, SSM_STATE), lambda j: (0, b_off + j // per_group)),
                  pl.BlockSpec((l, SSM_STATE), lambda j: (0, c_off + j // per_group)),
                  small, small, small, small,
                  pl.BlockSpec((1, 1, SSD_LANES), lambda j: (j, 0, 0))],
        out_specs=pl.BlockSpec((l, SSD_LANES), lambda j: (0, j)),
        out_shape=jax.ShapeDtypeStruct((l, SSM_D_INNER), BF16),
        scratch_shapes=[pltpu.VMEM((l // SSM_CHUNK, SSM_STATE, SSD_LANES), BF16),
                        pltpu.VMEM((SSM_STATE, SSD_LANES), F32)],
        compiler_params=_cparams(1),
        name="ssd_scan",
    )(xbc, xbc, xbc, dt_t, ac_t, tot_t, lg_t, d_lanes)


def _na_window(case, qr):
    if case == 0:
        return 0, NA_WIN_H, -qr
    if case == 1:
        return qr, qr + NA_WIN_H, -(NA_WIN_H // 2) - qr
    return NA_Q_ROWS, NA_BAND_ROWS, -NA_WIN_H - qr


def _na_kernel(rpb_ref, q_ref, k_ref, v_ref, z_ref, o_ref, t_ref, bias_ref, *, rows):
    h = pl.program_id(0)
    nblk = rows // NA_Q_ROWS
    gw = GRID_W
    tq = NA_Q_ROWS * gw
    band = NA_BAND_ROWS * gw
    n_dr = 2 * NA_WIN_H - 1
    n_dc = 2 * NA_WIN_W - 1

    kc = lax.broadcasted_iota(jnp.int32, (gw, 2 * gw), 0)
    lane = lax.broadcasted_iota(jnp.int32, (gw, 2 * gw), 1)
    qc = lane % gw
    diff = kc - qc + (NA_WIN_W - 1)
    cs = jnp.clip(qc - NA_WIN_W // 2, 0, gw - NA_WIN_W)
    col_ok = (kc >= cs) & (kc < cs + NA_WIN_W)
    for dr in range(n_dr):
        t = jnp.zeros((gw, 2 * gw), F32)
        for j in range(n_dc):
            t = jnp.where(diff == j, rpb_ref[(h * n_dr + dr) * n_dc + j] * LOG2E, t)
        t_ref[dr] = jnp.where(col_ok, t, NEG_BIG)
    neg = jnp.full((gw, 2 * gw), NEG_BIG, F32)
    for case in range(3):
        for kr in range(NA_BAND_ROWS):
            for pair in range(NA_Q_ROWS // 2):
                halves = []
                for qr in (2 * pair, 2 * pair + 1):
                    lo, hi, off = _na_window(case, qr)
                    halves.append(t_ref[kr + off + NA_WIN_H - 1] if lo <= kr < hi else neg)
                bias_ref[case, kr * gw:(kr + 1) * gw, pair * 2 * gw:(pair + 1) * 2 * gw] = (
                    jnp.where(lane < gw, halves[0], halves[1]))

    def scores(blk):
        _, q0, k0 = blk
        return lax.dot_general(k_ref[pl.ds(k0, band), :], q_ref[pl.ds(q0, tq), :],
                               (((1,), (1,)), ((), ())), preferred_element_type=F32)

    def finish(blk, s):
        case, q0, k0 = blk
        vb = v_ref[pl.ds(k0, band), :]
        ps, rden = [], []
        for pair in range(NA_Q_ROWS // 2):
            wins = [_na_window(case, qr) for qr in (2 * pair, 2 * pair + 1)]
            r0, r1 = min(w[0] for w in wins) * gw, max(w[1] for w in wins) * gw
            lanes = slice(pair * 2 * gw, (pair + 1) * 2 * gw)
            sq = s[r0:r1, lanes] + bias_ref[case, r0:r1, lanes]
            m = jnp.max(sq, axis=0, keepdims=True)
            p = jnp.exp2(sq - m)
            rden.append(1.0 / jnp.sum(p, axis=0, keepdims=True))
            parts = [p.astype(BF16)]
            if r0 > 0:
                parts.insert(0, jnp.zeros((r0, 2 * gw), BF16))
            if r1 < band:
                parts.append(jnp.zeros((band - r1, 2 * gw), BF16))
            ps.append(jnp.concatenate(parts, axis=0))
        o_t = lax.dot_general(vb, jnp.concatenate(ps, axis=1), (((0,), (0,)), ((), ())),
                              preferred_element_type=F32)
        o = (o_t * jnp.concatenate(rden, axis=1)).T
        z = z_ref[pl.ds(q0, tq), :].astype(F32)
        o_ref[pl.ds(q0, tq), :] = (o * (z * jax.nn.sigmoid(z))).astype(o_ref.dtype)

    def run_blocks(blocks):
        s = scores(blocks[0])
        for j, blk in enumerate(blocks):
            s_next = scores(blocks[j + 1]) if j + 1 < len(blocks) else None
            finish(blk, s)
            s = s_next

    def interior(i):
        q0 = i * tq
        if not isinstance(i, int):
            q0 = pl.multiple_of(q0, tq)
        return 1, q0, q0 - (NA_WIN_H // 2) * gw

    u = NA_BLOCKS_PER_TRIP
    top = (0, 0, 0)
    bottom = (2, (nblk - 1) * tq, (rows - NA_BAND_ROWS) * gw)
    run_blocks([top] + [interior(i) for i in range(1, u)])

    def trip(t, carry):
        run_blocks([interior(t * u + j) for j in range(u)])
        return carry

    lax.fori_loop(1, nblk // u - 1, trip, 0)
    run_blocks([interior(i) for i in range(nblk - u, nblk - 1)] + [bottom])


def _neighbourhood_attention(qkvz, rpb):
    l = qkvz.shape[0]
    rows = l // GRID_W
    nblk = rows // NA_Q_ROWS
    assert rows % NA_Q_ROWS == 0 and nblk % NA_BLOCKS_PER_TRIP == 0 and nblk >= 2 * NA_BLOCKS_PER_TRIP
    nh = NA_HEADS

    def col(part):
        return pl.BlockSpec((l, NA_HEAD_DIM), lambda h: (0, part * nh + h))

    return pl.pallas_call(
        functools.partial(_na_kernel, rows=rows),
        grid=(nh,),
        in_specs=[pl.BlockSpec(memory_space=pltpu.SMEM), col(0), col(1), col(2), col(3)],
        out_specs=col(0),
        out_shape=jax.ShapeDtypeStruct((l, NA_D_INNER), BF16),
        scratch_shapes=[pltpu.VMEM((2 * NA_WIN_H - 1, GRID_W, 2 * GRID_W), F32),
                        pltpu.VMEM((3, NA_BAND_ROWS * GRID_W, NA_Q_ROWS * GRID_W), F32)],
        compiler_params=_cparams(1),
        name="neighbourhood_attention",
    )(rpb.reshape(-1), qkvz, qkvz, qkvz, qkvz)


def _ssm_layer(h, norm_w, w_in, conv_w, conv_b, dt_bias, a_log, d_skip, gnorm_w, w_out):
    u = _rmsnorm(h, norm_w, BF16)
    n_zx = SSM_D_INNER + SSM_CONV_DIM
    zx = _matmul(u, w_in, n_zx, BF16, tm=2048, tn=1024)

    hp = HEADS_PER_STEP
    heads = jnp.arange(SSM_HEADS).reshape(SSM_HEADS // hp, 1, hp)
    perm = (heads + jnp.array([0, SSM_HEADS]).reshape(1, 2, 1)).reshape(-1)
    w_dt = w_in[:, n_zx:][:, perm]
    bias = dt_bias.reshape(-1)[perm].reshape(1, -1)
    alog = a_log.reshape(-1)[perm].reshape(-1, 1)
    dt_rows = _dt_project(u, w_dt, bias, alog)
    nsteps = SSM_HEADS // hp
    shp = (nsteps, 2 * hp, h.shape[0])

    xbc = _conv_silu(zx, conv_w, conv_b, SSM_D_INNER)
    d_lanes = jnp.repeat(d_skip, SSM_HEAD_DIM).reshape(nsteps, 1, SSD_LANES)
    y = _ssd_scan(xbc, *[r.reshape(shp) for r in dt_rows], d_lanes)
    g = _gated_rmsnorm(y, zx, gnorm_w)
    return _matmul(g, w_out, w_out.shape[1], F32, tm=512, tn=1024, res=h, single_buffer_w=True)


def _na_layer(h, norm_w, w_in, rpb, w_out):
    u = _rmsnorm(h, norm_w, BF16)
    n = w_in.shape[1]
    qscale = NA_HEAD_DIM ** -0.5 * LOG2E
    colscale = jnp.where(jnp.arange(n) < NA_D_INNER, qscale, 1.0).astype(F32).reshape(1, n)
    qkvz = _matmul(u, w_in, n, BF16, tm=2048, tn=1024, colscale=colscale)
    o = _neighbourhood_attention(qkvz, rpb)
    return _matmul(o, w_out, w_out.shape[1], F32, tm=512, tn=1024, res=h, single_buffer_w=True)


def kernel(x, norm_w, ssm_w_in, ssm_conv_w, ssm_conv_b, ssm_dt_bias, ssm_A_log, ssm_D, ssm_norm_w,
           ssm_w_out, na_w_in, na_rpb, na_w_out, final_norm_w):
    b, l, d = x.shape
    outs = []
    for bi in range(b):
        h = x[bi]
        h = _ssm_layer(h, norm_w[0], ssm_w_in[0], ssm_conv_w[0], ssm_conv_b[0], ssm_dt_bias[0],
                       ssm_A_log[0], ssm_D[0], ssm_norm_w[0], ssm_w_out[0])
        h = _na_layer(h, norm_w[1], na_w_in[0], na_rpb[0], na_w_out[0])
        outs.append(_rmsnorm(h, final_norm_w, x.dtype))
    return jnp.stack(outs, axis=0)
```

```python
import functools

import jax
import jax.numpy as jnp
from jax import lax
from jax.experimental import pallas as pl
from jax.experimental.pallas import tpu as pltpu

F32 = jnp.float32
BF16 = jnp.bfloat16

GRID_W = 64
SSM_HEAD_DIM = 64
SSM_HEADS = 64
SSM_GROUPS = 8
SSM_STATE = 128
SSM_CONV_W = 7
SSM_CHUNK = 128
SSM_D_INNER = SSM_HEADS * SSM_HEAD_DIM
SSM_CONV_DIM = SSM_D_INNER + 2 * SSM_GROUPS * SSM_STATE
NA_HEAD_DIM = 128
NA_HEADS = 32
NA_D_INNER = NA_HEADS * NA_HEAD_DIM
NA_WIN_H = 8
NA_WIN_W = 16
NORM_EPS = 1e-5

HEADS_PER_STEP = 4
SSD_LANES = HEADS_PER_STEP * SSM_HEAD_DIM
SSD_CHUNKS_PER_TRIP = 8
NA_Q_ROWS = 4
NA_BAND_ROWS = NA_Q_ROWS + NA_WIN_H
NA_BLOCKS_PER_TRIP = 8
LOG2E = 1.4426950408889634
NEG_BIG = -1e30
VMEM_LIMIT = 56 * 1024 * 1024


def _cparams(n_axes):
    return pltpu.CompilerParams(dimension_semantics=("arbitrary",) * n_axes,
                                vmem_limit_bytes=VMEM_LIMIT)


def _rmsnorm_kernel(x_ref, w_ref, o_ref):
    x = x_ref[...]
    ms = jnp.mean(x * x, axis=-1, keepdims=True)
    o_ref[...] = (x * lax.rsqrt(ms + NORM_EPS) * w_ref[...]).astype(o_ref.dtype)


def _rmsnorm(x, w, out_dtype, tm=1024):
    l, d = x.shape
    tm = min(tm, l)
    return pl.pallas_call(
        _rmsnorm_kernel,
        grid=(l // tm,),
        in_specs=[pl.BlockSpec((tm, d), lambda i: (i, 0)),
                  pl.BlockSpec((1, d), lambda i: (0, 0))],
        out_specs=pl.BlockSpec((tm, d), lambda i: (i, 0)),
        out_shape=jax.ShapeDtypeStruct((l, d), out_dtype),
        compiler_params=_cparams(1),
        name="rmsnorm",
    )(x, w.reshape(1, d))


GATED_K_SLICES = 8


def _gated_out_proj_kernel(y_ref, z_ref, w_ref, g_ref, r_ref, o_ref, wbf_ref):
    k = y_ref.shape[1]
    ks = k // GATED_K_SLICES

    @pl.when(pl.program_id(1) == 0)
    def _():
        wbf_ref[...] = w_ref[...].astype(BF16)

    acc = None
    ss = None
    for i in range(GATED_K_SLICES):
        cols = slice(i * ks, (i + 1) * ks)
        z = z_ref[:, cols].astype(F32)
        g = y_ref[:, cols].astype(F32) * (z * jax.nn.sigmoid(z))
        part = jnp.sum(g * g, axis=-1, keepdims=True)
        prod = jnp.dot((g * g_ref[:, cols]).astype(BF16), wbf_ref[cols, :], preferred_element_type=F32)
        acc = prod if acc is None else acc + prod
        ss = part if ss is None else ss + part
    o_ref[...] = r_ref[...] + acc * lax.rsqrt(ss * (1.0 / k) + NORM_EPS)


def _gated_out_proj(y, zx, gain, w, res, tm=512, tn=1024):
    l, d = y.shape
    n = w.shape[1]
    tm = min(tm, l)
    return pl.pallas_call(
        _gated_out_proj_kernel,
        grid=(n // tn, l // tm),
        in_specs=[pl.BlockSpec((tm, d), lambda j, i: (i, 0)),
                  pl.BlockSpec((tm, d), lambda j, i: (i, 0)),
                  pl.BlockSpec((d, tn), lambda j, i: (0, j), pipeline_mode=pl.Buffered(1)),
                  pl.BlockSpec((1, d), lambda j, i: (0, 0)),
                  pl.BlockSpec((tm, tn), lambda j, i: (i, j))],
        out_specs=pl.BlockSpec((tm, tn), lambda j, i: (i, j)),
        out_shape=jax.ShapeDtypeStruct((l, n), F32),
        scratch_shapes=[pltpu.VMEM((d, tn), BF16)],
        compiler_params=_cparams(2),
        name="gated_out_proj",
    )(y, zx, w, gain.reshape(1, d), res)


def _matmul_kernel(*refs, has_scale, has_res):
    x_ref, w_ref = refs[0], refs[1]
    pos = 2
    s_ref = r_ref = None
    if has_scale:
        s_ref = refs[pos]
        pos += 1
    if has_res:
        r_ref = refs[pos]
        pos += 1
    o_ref, wbf_ref = refs[pos], refs[pos + 1]

    @pl.when(pl.program_id(1) == 0)
    def _():
        wbf_ref[...] = w_ref[...].astype(BF16)

    acc = jnp.dot(x_ref[...], wbf_ref[...], preferred_element_type=F32)
    if has_scale:
        acc = acc * s_ref[...]
    if has_res:
        acc = acc + r_ref[...]
    o_ref[...] = acc.astype(o_ref.dtype)


def _matmul(x, w, n_out, out_dtype, tm, tn, colscale=None, res=None, single_buffer_w=False):
    l, k = x.shape
    tm = min(tm, l)
    assert l % tm == 0 and n_out % tn == 0
    w_mode = dict(pipeline_mode=pl.Buffered(1)) if single_buffer_w else {}
    in_specs = [pl.BlockSpec((tm, k), lambda n, m: (m, 0)),
                pl.BlockSpec((k, tn), lambda n, m: (0, n), **w_mode)]
    args = [x, w]
    if colscale is not None:
        in_specs.append(pl.BlockSpec((1, tn), lambda n, m: (0, n)))
        args.append(colscale)
    if res is not None:
        in_specs.append(pl.BlockSpec((tm, tn), lambda n, m: (m, n)))
        args.append(res)
    return pl.pallas_call(
        functools.partial(_matmul_kernel, has_scale=colscale is not None, has_res=res is not None),
        grid=(n_out // tn, l // tm),
        in_specs=in_specs,
        out_specs=pl.BlockSpec((tm, tn), lambda n, m: (m, n)),
        out_shape=jax.ShapeDtypeStruct((l, n_out), out_dtype),
        scratch_shapes=[pltpu.VMEM((k, tn), BF16)],
        compiler_params=_cparams(2),
        name="matmul",
    )(*args)


def _split3_dot(a, m_bf16):
    hi = a.astype(BF16)
    r1 = a - hi.astype(F32)
    mid = r1.astype(BF16)
    lo = (r1 - mid.astype(F32)).astype(BF16)
    out = jnp.dot(hi, m_bf16, preferred_element_type=F32)
    out = out + jnp.dot(mid, m_bf16, preferred_element_type=F32)
    return out + jnp.dot(lo, m_bf16, preferred_element_type=F32)


def _dt_kernel(u_ref, w_ref, bias_ref, alog_ref, dt_ref, ac_ref, tot_ref, lg_ref):
    tm = u_ref.shape[0]
    raw = jnp.dot(u_ref[...], w_ref[...].astype(BF16), preferred_element_type=F32)
    xb = raw + bias_ref[...]
    dt = jnp.maximum(xb, 0.0) + jnp.log1p(jnp.exp(-jnp.abs(xb)))
    dt_t = dt.T
    a = dt_t * (-jnp.exp(alog_ref[...]) * LOG2E)
    q = SSM_CHUNK
    si = lax.broadcasted_iota(jnp.int32, (q, q), 0)
    ti = lax.broadcasted_iota(jnp.int32, (q, q), 1)
    upper = jnp.where(si <= ti, 1.0, 0.0).astype(BF16)
    row = lax.broadcasted_iota(jnp.int32, (a.shape[0], q), 0)
    is_bwd = (row % (2 * HEADS_PER_STEP)) >= HEADS_PER_STEP
    dt_ref[...] = dt_t
    log2_dt = jnp.log2(dt_t)
    for c in range(tm // q):
        cols = slice(c * q, (c + 1) * q)
        a_c = a[:, cols]
        inc = _split3_dot(a_c, upper)
        tot_ref[:, cols] = inc
        ac = jnp.where(is_bwd, inc - a_c, inc)
        ac_ref[:, cols] = ac
        lg_ref[:, cols] = log2_dt[:, cols] + jnp.where(is_bwd, ac, -ac)


def _dt_project(u, w_dt, bias, alog, tm=512):
    l, k = u.shape
    n = w_dt.shape[1]
    out = jax.ShapeDtypeStruct((n, l), F32)
    return pl.pallas_call(
        _dt_kernel,
        grid=(l // tm,),
        in_specs=[pl.BlockSpec((tm, k), lambda i: (i, 0)),
                  pl.BlockSpec((k, n), lambda i: (0, 0)),
                  pl.BlockSpec((1, n), lambda i: (0, 0)),
                  pl.BlockSpec((n, 1), lambda i: (0, 0))],
        out_specs=[pl.BlockSpec((n, tm), lambda i: (0, i))] * 4,
        out_shape=[out] * 4,
        compiler_params=_cparams(1),
        name="dt_project",
    )(u, w_dt, bias, alog)


def _conv_silu_kernel(x_ref, w_ref, b_ref, o_ref, *, tile):
    l, cw = x_ref.shape
    pad = 16
    half = SSM_CONV_W // 2
    n_tiles = l // tile
    ext = tile + 2 * pad
    w = w_ref[...]
    b = b_ref[...]
    taps = [j for j in range(SSM_CONV_W) if j != half]
    ti = lax.broadcasted_iota(jnp.int32, (len(taps) * tile, ext), 0)
    ri = lax.broadcasted_iota(jnp.int32, (len(taps) * tile, ext), 1)
    tap_of_row = jnp.zeros_like(ti)
    for n, j in enumerate(taps):
        tap_of_row = jnp.where(ti // tile == n, j, tap_of_row)
    shift = jnp.where(ri == ti % tile + pad - half + tap_of_row, 1.0, 0.0).astype(BF16)

    def shifted_rows(i):
        r0 = pl.multiple_of(i * tile, tile)
        p0 = pl.multiple_of(jnp.maximum(r0 - pad, 0), pad)
        n0 = pl.multiple_of(jnp.minimum(r0 + tile, l - pad), pad)
        zero = jnp.zeros((pad, cw), BF16)
        prev = jnp.where(i > 0, x_ref[pl.ds(p0, pad), :], zero)
        nxt = jnp.where(i < n_tiles - 1, x_ref[pl.ds(n0, pad), :], zero)
        xe = jnp.concatenate([prev, x_ref[pl.ds(r0, tile), :], nxt], axis=0)
        return jnp.dot(shift, xe, preferred_element_type=F32)

    def body(i, carry):
        shifted = shifted_rows(i)
        r0 = pl.multiple_of(i * tile, tile)
        acc = b + x_ref[pl.ds(r0, tile), :].astype(F32) * w[half:half + 1, :]
        for n, j in enumerate(taps):
            acc = acc + shifted[n * tile:(n + 1) * tile, :] * w[j:j + 1, :]
        o_ref[pl.ds(r0, tile), :] = (acc * jax.nn.sigmoid(acc)).astype(o_ref.dtype)
        return carry

    lax.fori_loop(0, n_tiles, body, 0, unroll=min(16, n_tiles))


def _conv_silu(zx, conv_w, conv_b, col0, cw=256, tile=128):
    l = zx.shape[0]
    c = conv_w.shape[1]
    off = col0 // cw
    return pl.pallas_call(
        functools.partial(_conv_silu_kernel, tile=tile),
        grid=(c // cw,),
        in_specs=[pl.BlockSpec((l, cw), lambda j: (0, off + j)),
                  pl.BlockSpec((SSM_CONV_W, cw), lambda j: (0, j)),
                  pl.BlockSpec((1, cw), lambda j: (0, j))],
        out_specs=pl.BlockSpec((l, cw), lambda j: (0, j)),
        out_shape=jax.ShapeDtypeStruct((l, c), BF16),
        compiler_params=_cparams(1),
        name="conv_silu",
    )(zx, conv_w, conv_b.reshape(1, c))


def _head_selector(first_rows):
    n = len(first_rows)
    shape = (3 * 2 * HEADS_PER_STEP * n, SSD_LANES * n)
    row = lax.broadcasted_iota(jnp.int32, shape, 0)
    lane = lax.broadcasted_iota(jnp.int32, shape, 1)
    first = jnp.zeros(shape, jnp.int32)
    for i, f in enumerate(first_rows):
        first = jnp.where(lane // SSD_LANES == i, f, first)
    hit = (row // (3 * 2 * HEADS_PER_STEP) == lane // SSD_LANES) & (
        row % (2 * HEADS_PER_STEP) == (lane % SSD_LANES) // SSM_HEAD_DIM + first)
    return jnp.where(hit, 1.0, 0.0).astype(BF16)


def _rows_to_lanes(blocks, selector):
    pieces = []
    for r8 in blocks:
        hi = r8.astype(BF16).astype(F32)
        r1 = r8 - hi
        mid = r1.astype(BF16).astype(F32)
        pieces += [hi, mid, r1 - mid]
    lhs = jnp.concatenate(pieces, axis=0).astype(BF16)
    return lax.dot_general(lhs, selector, (((0,), (0,)), ((), ())), preferred_element_type=F32)


def _head_lane_vector(c4, lanehead):
    out = jnp.broadcast_to(c4[0:1, :], (1, SSD_LANES))
    for k in range(1, HEADS_PER_STEP):
        out = jnp.where(lanehead == k, jnp.broadcast_to(c4[k:k + 1, :], (1, SSD_LANES)), out)
    return out


def _ssd_kernel(x_ref, b_ref, c_ref, dt_ref, ac_ref, tot_ref, lg_ref, d_ref, y_ref, hf_ref, h_ref):
    l = x_ref.shape[0]
    q = SSM_CHUNK
    nc = l // q
    hp = HEADS_PER_STEP
    lanehead = lax.broadcasted_iota(jnp.int32, (1, SSD_LANES), 1) // SSM_HEAD_DIM
    ri = lax.broadcasted_iota(jnp.int32, (q, q), 0)
    ci = lax.broadcasted_iota(jnp.int32, (q, q), 1)
    contract0 = (((0,), (0,)), ((), ()))
    contract_last = (((1,), (1,)), ((), ()))
    sel_f = _head_selector([0])
    sel_fbb = _head_selector([0, hp, hp])
    u = SSD_CHUNKS_PER_TRIP

    def chunk_rows(c):
        return pl.ds(pl.multiple_of(c * q, q), q)


    h_ref[...] = jnp.zeros_like(h_ref)

    def fwd_trip(t, carry):
        chunks = [t * u + j for j in range(u)]
        a_lasts, wfs, ss = [], [], []
        for c in chunks:
            rows = chunk_rows(c)
            acr = ac_ref[0, :, rows]
            a_lasts.append(acr[:, q - 1:q])
            wfs.append(_rows_to_lanes([jnp.exp2(a_lasts[-1] - acr) * dt_ref[0, :, rows]], sel_f))
        for c, wf in zip(chunks, wfs):
            rows = chunk_rows(c)
            xw = (x_ref[rows, :].astype(F32) * wf).astype(BF16)
            ss.append(lax.dot_general(b_ref[rows, :], xw, contract0, preferred_element_type=F32))
        h = h_ref[...]
        for c, a_last, s in zip(chunks, a_lasts, ss):
            hf_ref[c] = h.astype(BF16)
            h = h * _head_lane_vector(jnp.exp2(a_last[0:hp]), lanehead) + s
        h_ref[...] = h
        return carry

    lax.fori_loop(0, nc // u, fwd_trip, 0)

    h_ref[...] = jnp.zeros_like(h_ref)

    def stage_a(c):
        rows = chunk_rows(c)
        bc, cc = b_ref[rows, :], c_ref[rows, :]
        dtr, acr = dt_ref[0, :, rows], ac_ref[0, :, rows]
        t_last = tot_ref[0, :, rows][:, q - 1:q]
        g = lax.dot_general(cc, bc, contract_last, preferred_element_type=F32)
        scales = _rows_to_lanes([jnp.exp2(acr), jnp.exp2(t_last - acr), jnp.exp2(acr) * dtr], sel_fbb)
        yoff_f = jnp.dot(cc, hf_ref[c], preferred_element_type=F32) * scales[:, 0:SSD_LANES]
        return dict(rows=rows, bc=bc, cc=cc, dtr=dtr, acr=acr, t_last=t_last, g=g, yoff_f=yoff_f,
                    scale_b=scales[:, SSD_LANES:2 * SSD_LANES], wb=scales[:, 2 * SSD_LANES:])

    def stage_b(d):
        xw = (x_ref[d["rows"], :].astype(F32) * d["wb"]).astype(BF16)
        d["s"] = lax.dot_general(d["bc"], xw, contract0, preferred_element_type=F32)

    def stage_c(d, hb):
        chb = jnp.dot(d["cc"], hb.astype(BF16), preferred_element_type=F32)
        hb = hb * _head_lane_vector(jnp.exp2(d["t_last"][hp:2 * hp]), lanehead) + d["s"]
        dtr, acr, g = d["dtr"], d["acr"], d["g"]
        lgr = lg_ref[0, :, d["rows"]]
        ncb = -acr[hp:2 * hp]
        dsum = dtr[0:hp] + dtr[hp:2 * hp]
        ms = []
        for k in range(hp):
            col_part = jnp.where(ri <= ci, acr[k:k + 1, :], ncb[k:k + 1, :]).T
            row_part = jnp.where(ci <= ri, lgr[k:k + 1, :], lgr[hp + k:hp + k + 1, :])
            e = jnp.where(ci == ri, dsum[k:k + 1, :], jnp.exp2(col_part + row_part))
            ms.append((g * e).astype(BF16))
        mcat = jnp.concatenate(ms, axis=1)
        xc = x_ref[d["rows"], :]
        xstack = jnp.concatenate(
            [jnp.where(lanehead == k, xc, jnp.zeros_like(xc)) for k in range(hp)], axis=0)
        y = jnp.dot(mcat, xstack, preferred_element_type=F32)
        y = y + d["yoff_f"] + chb * d["scale_b"] + d_ref[0] * xc.astype(F32)
        y_ref[d["rows"], :] = y.astype(y_ref.dtype)
        return hb

    def bwd_trip(t, carry):
        chunks = [nc - 1 - (t * u + j) for j in range(u)]
        st = [None] * u
        st[0] = stage_a(chunks[0])
        if u > 1:
            st[1] = stage_a(chunks[1])
        stage_b(st[0])
        hb = h_ref[...]
        for j in range(u):
            if j + 2 < u:
                st[j + 2] = stage_a(chunks[j + 2])
            if j + 1 < u:
                stage_b(st[j + 1])
            hb = stage_c(st[j], hb)
            st[j] = None
        h_ref[...] = hb
        return carry

    lax.fori_loop(0, nc // u, bwd_trip, 0)


def _ssd_scan(xbc, dt_t, ac_t, tot_t, lg_t, d_lanes):
    l = xbc.shape[0]
    assert l % (SSM_CHUNK * SSD_CHUNKS_PER_TRIP) == 0
    nsteps = SSM_HEADS // HEADS_PER_STEP
    per_group = SSM_HEADS // SSM_GROUPS // HEADS_PER_STEP
    b_off = SSM_D_INNER // SSM_STATE
    c_off = b_off + SSM_GROUPS
    small = pl.BlockSpec((1, 2 * HEADS_PER_STEP, l), lambda j: (j, 0, 0))
    return pl.pallas_call(
        _ssd_kernel,
        grid=(nsteps,),
        in_specs=[pl.BlockSpec((l, SSD_LANES), lambda j: (0, j)),
                  pl.BlockSpec((l, SSM_STATE), lambda j: (0, b_off + j // per_group)),
                  pl.BlockSpec((l, SSM_STATE), lambda j: (0, c_off + j // per_group)),
                  small, small, small, small,
                  pl.BlockSpec((1, 1, SSD_LANES), lambda j: (j, 0, 0))],
        out_specs=pl.BlockSpec((l, SSD_LANES), lambda j: (0, j)),
        out_shape=jax.ShapeDtypeStruct((l, SSM_D_INNER), BF16),
        scratch_shapes=[pltpu.VMEM((l // SSM_CHUNK, SSM_STATE, SSD_LANES), BF16),
                        pltpu.VMEM((SSM_STATE, SSD_LANES), F32)],
        compiler_params=_cparams(1),
        name="ssd_scan",
    )(xbc, xbc, xbc, dt_t, ac_t, tot_t, lg_t, d_lanes)


def _na_window(case, qr):
    if case == 0:
        return 0, NA_WIN_H, -qr
    if case == 1:
        return qr, qr + NA_WIN_H, -(NA_WIN_H // 2) - qr
    return NA_Q_ROWS, NA_BAND_ROWS, -NA_WIN_H - qr


def _na_kernel(rpb_ref, q_ref, k_ref, v_ref, z_ref, o_ref, t_ref, bias_ref, *, rows):
    h = pl.program_id(0)
    nblk = rows // NA_Q_ROWS
    gw = GRID_W
    tq = NA_Q_ROWS * gw
    band = NA_BAND_ROWS * gw
    n_dr = 2 * NA_WIN_H - 1
    n_dc = 2 * NA_WIN_W - 1

    kc = lax.broadcasted_iota(jnp.int32, (gw, 2 * gw), 0)
    lane = lax.broadcasted_iota(jnp.int32, (gw, 2 * gw), 1)
    qc = lane % gw
    diff = kc - qc + (NA_WIN_W - 1)
    cs = jnp.clip(qc - NA_WIN_W // 2, 0, gw - NA_WIN_W)
    col_ok = (kc >= cs) & (kc < cs + NA_WIN_W)
    for dr in range(n_dr):
        t = jnp.zeros((gw, 2 * gw), F32)
        for j in range(n_dc):
            t = jnp.where(diff == j, rpb_ref[(h * n_dr + dr) * n_dc + j] * LOG2E, t)
        t_ref[dr] = jnp.where(col_ok, t, NEG_BIG)
    neg = jnp.full((gw, 2 * gw), NEG_BIG, F32)
    for case in range(3):
        for kr in range(NA_BAND_ROWS):
            for pair in range(NA_Q_ROWS // 2):
                halves = []
                for qr in (2 * pair, 2 * pair + 1):
                    lo, hi, off = _na_window(case, qr)
                    halves.append(t_ref[kr + off + NA_WIN_H - 1] if lo <= kr < hi else neg)
                bias_ref[case, kr * gw:(kr + 1) * gw, pair * 2 * gw:(pair + 1) * 2 * gw] = (
                    jnp.where(lane < gw, halves[0], halves[1]))

    def scores(blk):
        _, q0, k0 = blk
        return lax.dot_general(k_ref[pl.ds(k0, band), :], q_ref[pl.ds(q0, tq), :],
                               (((1,), (1,)), ((), ())), preferred_element_type=F32)

    def finish(blk, s):
        case, q0, k0 = blk
        vb = v_ref[pl.ds(k0, band), :]
        ps, rden = [], []
        for pair in range(NA_Q_ROWS // 2):
            wins = [_na_window(case, qr) for qr in (2 * pair, 2 * pair + 1)]
            r0, r1 = min(w[0] for w in wins) * gw, max(w[1] for w in wins) * gw
            lanes = slice(pair * 2 * gw, (pair + 1) * 2 * gw)
            sq = s[r0:r1, lanes] + bias_ref[case, r0:r1, lanes]
            m = jnp.max(sq, axis=0, keepdims=True)
            p = jnp.exp2(sq - m)
            rden.append(1.0 / jnp.sum(p, axis=0, keepdims=True))
            parts = [p.astype(BF16)]
            if r0 > 0:
                parts.insert(0, jnp.zeros((r0, 2 * gw), BF16))
            if r1 < band:
                parts.append(jnp.zeros((band - r1, 2 * gw), BF16))
            ps.append(jnp.concatenate(parts, axis=0))
        o_t = lax.dot_general(vb, jnp.concatenate(ps, axis=1), (((0,), (0,)), ((), ())),
                              preferred_element_type=F32)
        o = (o_t * jnp.concatenate(rden, axis=1)).T
        z = z_ref[pl.ds(q0, tq), :].astype(F32)
        o_ref[pl.ds(q0, tq), :] = (o * (z * jax.nn.sigmoid(z))).astype(o_ref.dtype)

    def run_blocks(blocks):
        s = scores(blocks[0])
        for j, blk in enumerate(blocks):
            s_next = scores(blocks[j + 1]) if j + 1 < len(blocks) else None
            finish(blk, s)
            s = s_next

    def interior(i):
        q0 = i * tq
        if not isinstance(i, int):
            q0 = pl.multiple_of(q0, tq)
        return 1, q0, q0 - (NA_WIN_H // 2) * gw

    u = NA_BLOCKS_PER_TRIP
    top = (0, 0, 0)
    bottom = (2, (nblk - 1) * tq, (rows - NA_BAND_ROWS) * gw)
    run_blocks([top] + [interior(i) for i in range(1, u)])

    def trip(t, carry):
        run_blocks([interior(t * u + j) for j in range(u)])
        return carry

    lax.fori_loop(1, nblk // u - 1, trip, 0)
    run_blocks([interior(i) for i in range(nblk - u, nblk - 1)] + [bottom])


def _neighbourhood_attention(qkvz, rpb):
    l = qkvz.shape[0]
    rows = l // GRID_W
    nblk = rows // NA_Q_ROWS
    assert rows % NA_Q_ROWS == 0 and nblk % NA_BLOCKS_PER_TRIP == 0 and nblk >= 2 * NA_BLOCKS_PER_TRIP
    nh = NA_HEADS

    def col(part):
        return pl.BlockSpec((l, NA_HEAD_DIM), lambda h: (0, part * nh + h))

    return pl.pallas_call(
        functools.partial(_na_kernel, rows=rows),
        grid=(nh,),
        in_specs=[pl.BlockSpec(memory_space=pltpu.SMEM), col(0), col(1), col(2), col(3)],
        out_specs=col(0),
        out_shape=jax.ShapeDtypeStruct((l, NA_D_INNER), BF16),
        scratch_shapes=[pltpu.VMEM((2 * NA_WIN_H - 1, GRID_W, 2 * GRID_W), F32),
                        pltpu.VMEM((3, NA_BAND_ROWS * GRID_W, NA_Q_ROWS * GRID_W), F32)],
        compiler_params=_cparams(1),
        name="neighbourhood_attention",
    )(rpb.reshape(-1), qkvz, qkvz, qkvz, qkvz)


def _ssm_layer(h, norm_w, w_in, conv_w, conv_b, dt_bias, a_log, d_skip, gnorm_w, w_out):
    u = _rmsnorm(h, norm_w, BF16)
    n_zx = SSM_D_INNER + SSM_CONV_DIM
    zx = _matmul(u, w_in, n_zx, BF16, tm=2048, tn=1024)

    hp = HEADS_PER_STEP
    heads = jnp.arange(SSM_HEADS).reshape(SSM_HEADS // hp, 1, hp)
    perm = (heads + jnp.array([0, SSM_HEADS]).reshape(1, 2, 1)).reshape(-1)
    w_dt = w_in[:, n_zx:][:, perm]
    bias = dt_bias.reshape(-1)[perm].reshape(1, -1)
    alog = a_log.reshape(-1)[perm].reshape(-1, 1)
    dt_rows = _dt_project(u, w_dt, bias, alog)
    nsteps = SSM_HEADS // hp
    shp = (nsteps, 2 * hp, h.shape[0])

    xbc = _conv_silu(zx, conv_w, conv_b, SSM_D_INNER)
    d_lanes = jnp.repeat(d_skip, SSM_HEAD_DIM).reshape(nsteps, 1, SSD_LANES)
    y = _ssd_scan(xbc, *[r.reshape(shp) for r in dt_rows], d_lanes)
    return _gated_out_proj(y, zx, gnorm_w, w_out, h)


def _na_layer(h, norm_w, w_in, rpb, w_out):
    u = _rmsnorm(h, norm_w, BF16)
    n = w_in.shape[1]
    qscale = NA_HEAD_DIM ** -0.5 * LOG2E
    colscale = jnp.where(jnp.arange(n) < NA_D_INNER, qscale, 1.0).astype(F32).reshape(1, n)
    qkvz = _matmul(u, w_in, n, BF16, tm=2048, tn=1024, colscale=colscale)
    o = _neighbourhood_attention(qkvz, rpb)
    return _matmul(o, w_out, w_out.shape[1], F32, tm=512, tn=1024, res=h, single_buffer_w=True)


def kernel(x, norm_w, ssm_w_in, ssm_conv_w, ssm_conv_b, ssm_dt_bias, ssm_A_log, ssm_D, ssm_norm_w,
           ssm_w_out, na_w_in, na_rpb, na_w_out, final_norm_w):
    b, l, d = x.shape
    outs = []
    for bi in range(b):
        h = x[bi]
        h = _ssm_layer(h, norm_w[0], ssm_w_in[0], ssm_conv_w[0], ssm_conv_b[0], ssm_dt_bias[0],
                       ssm_A_log[0], ssm_D[0], ssm_norm_w[0], ssm_w_out[0])
        h = _na_layer(h, norm_w[1], na_w_in[0], na_rpb[0], na_w_out[0])
        outs.append(_rmsnorm(h, final_norm_w, x.dtype))
    return jnp.stack(outs, axis=0)
```

```python
import functools

import jax
import jax.numpy as jnp
from jax import lax
from jax.experimental import pallas as pl
from jax.experimental.pallas import tpu as pltpu

F32 = jnp.float32
BF16 = jnp.bfloat16

GRID_W = 64
SSM_HEAD_DIM = 64
SSM_HEADS = 64
SSM_GROUPS = 8
SSM_STATE = 128
SSM_CONV_W = 7
SSM_CHUNK = 128
SSM_D_INNER = SSM_HEADS * SSM_HEAD_DIM
SSM_CONV_DIM = SSM_D_INNER + 2 * SSM_GROUPS * SSM_STATE
NA_HEAD_DIM = 128
NA_HEADS = 32
NA_D_INNER = NA_HEADS * NA_HEAD_DIM
NA_WIN_H = 8
NA_WIN_W = 16
NORM_EPS = 1e-5

HEADS_PER_STEP = 4
SSD_LANES = HEADS_PER_STEP * SSM_HEAD_DIM
SSD_CHUNKS_PER_TRIP = 8
NA_Q_ROWS = 4
NA_BAND_ROWS = NA_Q_ROWS + NA_WIN_H
NA_BLOCKS_PER_TRIP = 8
LOG2E = 1.4426950408889634
NEG_BIG = -1e30
VMEM_LIMIT = 56 * 1024 * 1024


def _cparams(n_axes):
    return pltpu.CompilerParams(dimension_semantics=("arbitrary",) * n_axes,
                                vmem_limit_bytes=VMEM_LIMIT)


def _rmsnorm_kernel(x_ref, w_ref, o_ref):
    x = x_ref[...]
    ms = jnp.mean(x * x, axis=-1, keepdims=True)
    o_ref[...] = (x * lax.rsqrt(ms + NORM_EPS) * w_ref[...]).astype(o_ref.dtype)


def _rmsnorm(x, w, out_dtype, tm=1024):
    l, d = x.shape
    tm = min(tm, l)
    return pl.pallas_call(
        _rmsnorm_kernel,
        grid=(l // tm,),
        in_specs=[pl.BlockSpec((tm, d), lambda i: (i, 0)),
                  pl.BlockSpec((1, d), lambda i: (0, 0))],
        out_specs=pl.BlockSpec((tm, d), lambda i: (i, 0)),
        out_shape=jax.ShapeDtypeStruct((l, d), out_dtype),
        compiler_params=_cparams(1),
        name="rmsnorm",
    )(x, w.reshape(1, d))


GATED_K_SLICES = 8


def _gated_out_proj_kernel(y_ref, z_ref, w_ref, g_ref, r_ref, o_ref, wbf_ref):
    k = y_ref.shape[1]
    ks = k // GATED_K_SLICES

    @pl.when(pl.program_id(1) == 0)
    def _():
        wbf_ref[...] = w_ref[...].astype(BF16)

    acc = None
    ss = None
    for i in range(GATED_K_SLICES):
        cols = slice(i * ks, (i + 1) * ks)
        z = z_ref[:, cols].astype(F32)
        g = y_ref[:, cols].astype(F32) * (z * jax.nn.sigmoid(z))
        part = jnp.sum(g * g, axis=-1, keepdims=True)
        prod = jnp.dot((g * g_ref[:, cols]).astype(BF16), wbf_ref[cols, :], preferred_element_type=F32)
        acc = prod if acc is None else acc + prod
        ss = part if ss is None else ss + part
    o_ref[...] = r_ref[...] + acc * lax.rsqrt(ss * (1.0 / k) + NORM_EPS)


def _gated_out_proj(y, zx, gain, w, res, tm=512, tn=1024):
    l, d = y.shape
    n = w.shape[1]
    tm = min(tm, l)
    return pl.pallas_call(
        _gated_out_proj_kernel,
        grid=(n // tn, l // tm),
        in_specs=[pl.BlockSpec((tm, d), lambda j, i: (i, 0)),
                  pl.BlockSpec((tm, d), lambda j, i: (i, 0)),
                  pl.BlockSpec((d, tn), lambda j, i: (0, j), pipeline_mode=pl.Buffered(1)),
                  pl.BlockSpec((1, d), lambda j, i: (0, 0)),
                  pl.BlockSpec((tm, tn), lambda j, i: (i, j))],
        out_specs=pl.BlockSpec((tm, tn), lambda j, i: (i, j)),
        out_shape=jax.ShapeDtypeStruct((l, n), F32),
        scratch_shapes=[pltpu.VMEM((d, tn), BF16)],
        compiler_params=_cparams(2),
        name="gated_out_proj",
    )(y, zx, w, gain.reshape(1, d), res)


def _matmul_kernel(*refs, has_scale, has_res):
    x_ref, w_ref = refs[0], refs[1]
    pos = 2
    s_ref = r_ref = None
    if has_scale:
        s_ref = refs[pos]
        pos += 1
    if has_res:
        r_ref = refs[pos]
        pos += 1
    o_ref, wbf_ref = refs[pos], refs[pos + 1]

    @pl.when(pl.program_id(1) == 0)
    def _():
        wbf_ref[...] = w_ref[...].astype(BF16)

    acc = jnp.dot(x_ref[...], wbf_ref[...], preferred_element_type=F32)
    if has_scale:
        acc = acc * s_ref[...]
    if has_res:
        acc = acc + r_ref[...]
    o_ref[...] = acc.astype(o_ref.dtype)


def _matmul(x, w, n_out, out_dtype, tm, tn, colscale=None, res=None, single_buffer_w=False):
    l, k = x.shape
    tm = min(tm, l)
    assert l % tm == 0 and n_out % tn == 0
    w_mode = dict(pipeline_mode=pl.Buffered(1)) if single_buffer_w else {}
    in_specs = [pl.BlockSpec((tm, k), lambda n, m: (m, 0)),
                pl.BlockSpec((k, tn), lambda n, m: (0, n), **w_mode)]
    args = [x, w]
    if colscale is not None:
        in_specs.append(pl.BlockSpec((1, tn), lambda n, m: (0, n)))
        args.append(colscale)
    if res is not None:
        in_specs.append(pl.BlockSpec((tm, tn), lambda n, m: (m, n)))
        args.append(res)
    return pl.pallas_call(
        functools.partial(_matmul_kernel, has_scale=colscale is not None, has_res=res is not None),
        grid=(n_out // tn, l // tm),
        in_specs=in_specs,
        out_specs=pl.BlockSpec((tm, tn), lambda n, m: (m, n)),
        out_shape=jax.ShapeDtypeStruct((l, n_out), out_dtype),
        scratch_shapes=[pltpu.VMEM((k, tn), BF16)],
        compiler_params=_cparams(2),
        name="matmul",
    )(*args)


def _split3_dot(a, m_bf16):
    hi = a.astype(BF16)
    r1 = a - hi.astype(F32)
    mid = r1.astype(BF16)
    lo = (r1 - mid.astype(F32)).astype(BF16)
    out = jnp.dot(hi, m_bf16, preferred_element_type=F32)
    out = out + jnp.dot(mid, m_bf16, preferred_element_type=F32)
    return out + jnp.dot(lo, m_bf16, preferred_element_type=F32)


def _dt_kernel(x_ref, g_ref, w_ref, bias_ref, alog_ref, u_ref, dt_ref, ac_ref, tot_ref, lg_ref):
    tm = x_ref.shape[0]
    x = x_ref[...]
    u = (x * lax.rsqrt(jnp.mean(x * x, axis=-1, keepdims=True) + NORM_EPS) * g_ref[...]).astype(BF16)
    u_ref[...] = u
    raw = jnp.dot(u, w_ref[...].astype(BF16), preferred_element_type=F32)
    xb = raw + bias_ref[...]
    dt = jnp.maximum(xb, 0.0) + jnp.log1p(jnp.exp(-jnp.abs(xb)))
    dt_t = dt.T
    a = dt_t * (-jnp.exp(alog_ref[...]) * LOG2E)
    q = SSM_CHUNK
    si = lax.broadcasted_iota(jnp.int32, (q, q), 0)
    ti = lax.broadcasted_iota(jnp.int32, (q, q), 1)
    upper = jnp.where(si <= ti, 1.0, 0.0).astype(BF16)
    row = lax.broadcasted_iota(jnp.int32, (a.shape[0], q), 0)
    is_bwd = (row % (2 * HEADS_PER_STEP)) >= HEADS_PER_STEP
    dt_ref[...] = dt_t
    log2_dt = jnp.log2(dt_t)
    for c in range(tm // q):
        cols = slice(c * q, (c + 1) * q)
        a_c = a[:, cols]
        inc = _split3_dot(a_c, upper)
        tot_ref[:, cols] = inc
        ac = jnp.where(is_bwd, inc - a_c, inc)
        ac_ref[:, cols] = ac
        lg_ref[:, cols] = log2_dt[:, cols] + jnp.where(is_bwd, ac, -ac)


def _norm_dt_project(x, gain, w_dt, bias, alog, tm=1024):
    l, k = x.shape
    n = w_dt.shape[1]
    tm = min(tm, l)
    out = jax.ShapeDtypeStruct((n, l), F32)
    return pl.pallas_call(
        _dt_kernel,
        grid=(l // tm,),
        in_specs=[pl.BlockSpec((tm, k), lambda i: (i, 0)),
                  pl.BlockSpec((1, k), lambda i: (0, 0)),
                  pl.BlockSpec((k, n), lambda i: (0, 0)),
                  pl.BlockSpec((1, n), lambda i: (0, 0)),
                  pl.BlockSpec((n, 1), lambda i: (0, 0))],
        out_specs=[pl.BlockSpec((tm, k), lambda i: (i, 0))] + [pl.BlockSpec((n, tm), lambda i: (0, i))] * 4,
        out_shape=[jax.ShapeDtypeStruct((l, k), BF16)] + [out] * 4,
        compiler_params=_cparams(1),
        name="norm_dt_project",
    )(x, gain.reshape(1, k), w_dt, bias, alog)


def _conv_silu_kernel(x_ref, w_ref, b_ref, o_ref, *, tile):
    l, cw = x_ref.shape
    pad = 16
    half = SSM_CONV_W // 2
    n_tiles = l // tile
    ext = tile + 2 * pad
    w = w_ref[...]
    b = b_ref[...]
    taps = [j for j in range(SSM_CONV_W) if j != half]
    ti = lax.broadcasted_iota(jnp.int32, (len(taps) * tile, ext), 0)
    ri = lax.broadcasted_iota(jnp.int32, (len(taps) * tile, ext), 1)
    tap_of_row = jnp.zeros_like(ti)
    for n, j in enumerate(taps):
        tap_of_row = jnp.where(ti // tile == n, j, tap_of_row)
    shift = jnp.where(ri == ti % tile + pad - half + tap_of_row, 1.0, 0.0).astype(BF16)

    def shifted_rows(i):
        r0 = pl.multiple_of(i * tile, tile)
        p0 = pl.multiple_of(jnp.maximum(r0 - pad, 0), pad)
        n0 = pl.multiple_of(jnp.minimum(r0 + tile, l - pad), pad)
        zero = jnp.zeros((pad, cw), BF16)
        prev = jnp.where(i > 0, x_ref[pl.ds(p0, pad), :], zero)
        nxt = jnp.where(i < n_tiles - 1, x_ref[pl.ds(n0, pad), :], zero)
        xe = jnp.concatenate([prev, x_ref[pl.ds(r0, tile), :], nxt], axis=0)
        return jnp.dot(shift, xe, preferred_element_type=F32)

    def body(i, carry):
        shifted = shifted_rows(i)
        r0 = pl.multiple_of(i * tile, tile)
        acc = b + x_ref[pl.ds(r0, tile), :].astype(F32) * w[half:half + 1, :]
        for n, j in enumerate(taps):
            acc = acc + shifted[n * tile:(n + 1) * tile, :] * w[j:j + 1, :]
        o_ref[pl.ds(r0, tile), :] = (acc * jax.nn.sigmoid(acc)).astype(o_ref.dtype)
        return carry

    lax.fori_loop(0, n_tiles, body, 0, unroll=min(16, n_tiles))


def _conv_silu(zx, conv_w, conv_b, col0, cw=256, tile=128):
    l = zx.shape[0]
    c = conv_w.shape[1]
    off = col0 // cw
    return pl.pallas_call(
        functools.partial(_conv_silu_kernel, tile=tile),
        grid=(c // cw,),
        in_specs=[pl.BlockSpec((l, cw), lambda j: (0, off + j)),
                  pl.BlockSpec((SSM_CONV_W, cw), lambda j: (0, j)),
                  pl.BlockSpec((1, cw), lambda j: (0, j))],
        out_specs=pl.BlockSpec((l, cw), lambda j: (0, j)),
        out_shape=jax.ShapeDtypeStruct((l, c), BF16),
        compiler_params=_cparams(1),
        name="conv_silu",
    )(zx, conv_w, conv_b.reshape(1, c))


def _head_selector(first_rows):
    n = len(first_rows)
    shape = (3 * 2 * HEADS_PER_STEP * n, SSD_LANES * n)
    row = lax.broadcasted_iota(jnp.int32, shape, 0)
    lane = lax.broadcasted_iota(jnp.int32, shape, 1)
    first = jnp.zeros(shape, jnp.int32)
    for i, f in enumerate(first_rows):
        first = jnp.where(lane // SSD_LANES == i, f, first)
    hit = (row // (3 * 2 * HEADS_PER_STEP) == lane // SSD_LANES) & (
        row % (2 * HEADS_PER_STEP) == (lane % SSD_LANES) // SSM_HEAD_DIM + first)
    return jnp.where(hit, 1.0, 0.0).astype(BF16)


def _rows_to_lanes(blocks, selector):
    pieces = []
    for r8 in blocks:
        hi = r8.astype(BF16).astype(F32)
        r1 = r8 - hi
        mid = r1.astype(BF16).astype(F32)
        pieces += [hi, mid, r1 - mid]
    lhs = jnp.concatenate(pieces, axis=0).astype(BF16)
    return lax.dot_general(lhs, selector, (((0,), (0,)), ((), ())), preferred_element_type=F32)


def _head_lane_vector(c4, lanehead):
    out = jnp.broadcast_to(c4[0:1, :], (1, SSD_LANES))
    for k in range(1, HEADS_PER_STEP):
        out = jnp.where(lanehead == k, jnp.broadcast_to(c4[k:k + 1, :], (1, SSD_LANES)), out)
    return out


def _ssd_kernel(x_ref, b_ref, c_ref, dt_ref, ac_ref, tot_ref, lg_ref, d_ref, y_ref, hf_ref, h_ref):
    l = x_ref.shape[0]
    q = SSM_CHUNK
    nc = l // q
    hp = HEADS_PER_STEP
    lanehead = lax.broadcasted_iota(jnp.int32, (1, SSD_LANES), 1) // SSM_HEAD_DIM
    ri = lax.broadcasted_iota(jnp.int32, (q, q), 0)
    ci = lax.broadcasted_iota(jnp.int32, (q, q), 1)
    contract0 = (((0,), (0,)), ((), ()))
    contract_last = (((1,), (1,)), ((), ()))
    sel_f = _head_selector([0])
    sel_fbb = _head_selector([0, hp, hp])
    u = SSD_CHUNKS_PER_TRIP

    def chunk_rows(c):
        return pl.ds(pl.multiple_of(c * q, q), q)


    h_ref[...] = jnp.zeros_like(h_ref)

    def fwd_trip(t, carry):
        chunks = [t * u + j for j in range(u)]
        a_lasts, wfs, ss = [], [], []
        for c in chunks:
            rows = chunk_rows(c)
            acr = ac_ref[0, :, rows]
            a_lasts.append(acr[:, q - 1:q])
            wfs.append(_rows_to_lanes([jnp.exp2(a_lasts[-1] - acr) * dt_ref[0, :, rows]], sel_f))
        for c, wf in zip(chunks, wfs):
            rows = chunk_rows(c)
            xw = (x_ref[rows, :].astype(F32) * wf).astype(BF16)
            ss.append(lax.dot_general(b_ref[rows, :], xw, contract0, preferred_element_type=F32))
        h = h_ref[...]
        for c, a_last, s in zip(chunks, a_lasts, ss):
            hf_ref[c] = h.astype(BF16)
            h = h * _head_lane_vector(jnp.exp2(a_last[0:hp]), lanehead) + s
        h_ref[...] = h
        return carry

    lax.fori_loop(0, nc // u, fwd_trip, 0)

    h_ref[...] = jnp.zeros_like(h_ref)

    def stage_a(c):
        rows = chunk_rows(c)
        bc, cc = b_ref[rows, :], c_ref[rows, :]
        dtr, acr = dt_ref[0, :, rows], ac_ref[0, :, rows]
        t_last = tot_ref[0, :, rows][:, q - 1:q]
        g = lax.dot_general(cc, bc, contract_last, preferred_element_type=F32)
        scales = _rows_to_lanes([jnp.exp2(acr), jnp.exp2(t_last - acr), jnp.exp2(acr) * dtr], sel_fbb)
        yoff_f = jnp.dot(cc, hf_ref[c], preferred_element_type=F32) * scales[:, 0:SSD_LANES]
        return dict(rows=rows, bc=bc, cc=cc, dtr=dtr, acr=acr, t_last=t_last, g=g, yoff_f=yoff_f,
                    scale_b=scales[:, SSD_LANES:2 * SSD_LANES], wb=scales[:, 2 * SSD_LANES:])

    def stage_b(d):
        xw = (x_ref[d["rows"], :].astype(F32) * d["wb"]).astype(BF16)
        d["s"] = lax.dot_general(d["bc"], xw, contract0, preferred_element_type=F32)

    def stage_c(d, hb):
        chb = jnp.dot(d["cc"], hb.astype(BF16), preferred_element_type=F32)
        hb = hb * _head_lane_vector(jnp.exp2(d["t_last"][hp:2 * hp]), lanehead) + d["s"]
        dtr, acr, g = d["dtr"], d["acr"], d["g"]
        lgr = lg_ref[0, :, d["rows"]]
        ncb = -acr[hp:2 * hp]
        dsum = dtr[0:hp] + dtr[hp:2 * hp]
        ms = []
        for k in range(hp):
            col_part = jnp.where(ri <= ci, acr[k:k + 1, :], ncb[k:k + 1, :]).T
            row_part = jnp.where(ci <= ri, lgr[k:k + 1, :], lgr[hp + k:hp + k + 1, :])
            e = jnp.where(ci == ri, dsum[k:k + 1, :], jnp.exp2(col_part + row_part))
            ms.append((g * e).astype(BF16))
        mcat = jnp.concatenate(ms, axis=1)
        xc = x_ref[d["rows"], :]
        xstack = jnp.concatenate(
            [jnp.where(lanehead == k, xc, jnp.zeros_like(xc)) for k in range(hp)], axis=0)
        y = jnp.dot(mcat, xstack, preferred_element_type=F32)
        y = y + d["yoff_f"] + chb * d["scale_b"] + d_ref[0] * xc.astype(F32)
        y_ref[d["rows"], :] = y.astype(y_ref.dtype)
        return hb

    def bwd_trip(t, carry):
        chunks = [nc - 1 - (t * u + j) for j in range(u)]
        st = [None] * u
        st[0] = stage_a(chunks[0])
        if u > 1:
            st[1] = stage_a(chunks[1])
        stage_b(st[0])
        hb = h_ref[...]
        for j in range(u):
            if j + 2 < u:
                st[j + 2] = stage_a(chunks[j + 2])
            if j + 1 < u:
                stage_b(st[j + 1])
            hb = stage_c(st[j], hb)
            st[j] = None
        h_ref[...] = hb
        return carry

    lax.fori_loop(0, nc // u, bwd_trip, 0)


def _ssd_scan(xbc, dt_t, ac_t, tot_t, lg_t, d_lanes):
    l = xbc.shape[0]
    assert l % (SSM_CHUNK * SSD_CHUNKS_PER_TRIP) == 0
    nsteps = SSM_HEADS // HEADS_PER_STEP
    per_group = SSM_HEADS // SSM_GROUPS // HEADS_PER_STEP
    b_off = SSM_D_INNER // SSM_STATE
    c_off = b_off + SSM_GROUPS
    small = pl.BlockSpec((1, 2 * HEADS_PER_STEP, l), lambda j: (j, 0, 0))
    return pl.pallas_call(
        _ssd_kernel,
        grid=(nsteps,),
        in_specs=[pl.BlockSpec((l, SSD_LANES), lambda j: (0, j)),
                  pl.BlockSpec((l, SSM_STATE), lambda j: (0, b_off + j // per_group)),
                  pl.BlockSpec((l, SSM_STATE), lambda j: (0, c_off + j // per_group)),
                  small, small, small, small,
                  pl.BlockSpec((1, 1, SSD_LANES), lambda j: (j, 0, 0))],
        out_specs=pl.BlockSpec((l, SSD_LANES), lambda j: (0, j)),
        out_shape=jax.ShapeDtypeStruct((l, SSM_D_INNER), BF16),
        scratch_shapes=[pltpu.VMEM((l // SSM_CHUNK, SSM_STATE, SSD_LANES), BF16),
                        pltpu.VMEM((SSM_STATE, SSD_LANES), F32)],
        compiler_params=_cparams(1),
        name="ssd_scan",
    )(xbc, xbc, xbc, dt_t, ac_t, tot_t, lg_t, d_lanes)


def _na_window(case, qr):
    if case == 0:
        return 0, NA_WIN_H, -qr
    if case == 1:
        return qr, qr + NA_WIN_H, -(NA_WIN_H // 2) - qr
    return NA_Q_ROWS, NA_BAND_ROWS, -NA_WIN_H - qr


def _na_kernel(rpb_ref, q_ref, k_ref, v_ref, z_ref, o_ref, t_ref, bias_ref, *, rows):
    h = pl.program_id(0)
    nblk = rows // NA_Q_ROWS
    gw = GRID_W
    tq = NA_Q_ROWS * gw
    band = NA_BAND_ROWS * gw
    n_dr = 2 * NA_WIN_H - 1
    n_dc = 2 * NA_WIN_W - 1

    kc = lax.broadcasted_iota(jnp.int32, (gw, 2 * gw), 0)
    lane = lax.broadcasted_iota(jnp.int32, (gw, 2 * gw), 1)
    qc = lane % gw
    diff = kc - qc + (NA_WIN_W - 1)
    cs = jnp.clip(qc - NA_WIN_W // 2, 0, gw - NA_WIN_W)
    col_ok = (kc >= cs) & (kc < cs + NA_WIN_W)
    for dr in range(n_dr):
        t = jnp.zeros((gw, 2 * gw), F32)
        for j in range(n_dc):
            t = jnp.where(diff == j, rpb_ref[(h * n_dr + dr) * n_dc + j] * LOG2E, t)
        t_ref[dr] = jnp.where(col_ok, t, NEG_BIG)
    neg = jnp.full((gw, 2 * gw), NEG_BIG, F32)
    for case in range(3):
        for kr in range(NA_BAND_ROWS):
            for pair in range(NA_Q_ROWS // 2):
                halves = []
                for qr in (2 * pair, 2 * pair + 1):
                    lo, hi, off = _na_window(case, qr)
                    halves.append(t_ref[kr + off + NA_WIN_H - 1] if lo <= kr < hi else neg)
                bias_ref[case, kr * gw:(kr + 1) * gw, pair * 2 * gw:(pair + 1) * 2 * gw] = (
                    jnp.where(lane < gw, halves[0], halves[1]))

    def scores(blk):
        _, q0, k0 = blk
        return lax.dot_general(k_ref[pl.ds(k0, band), :], q_ref[pl.ds(q0, tq), :],
                               (((1,), (1,)), ((), ())), preferred_element_type=F32)

    def finish(blk, s):
        case, q0, k0 = blk
        vb = v_ref[pl.ds(k0, band), :]
        ps, rden = [], []
        for pair in range(NA_Q_ROWS // 2):
            wins = [_na_window(case, qr) for qr in (2 * pair, 2 * pair + 1)]
            r0, r1 = min(w[0] for w in wins) * gw, max(w[1] for w in wins) * gw
            lanes = slice(pair * 2 * gw, (pair + 1) * 2 * gw)
            sq = s[r0:r1, lanes] + bias_ref[case, r0:r1, lanes]
            m = jnp.max(sq, axis=0, keepdims=True)
            p = jnp.exp2(sq - m)
            rden.append(1.0 / jnp.sum(p, axis=0, keepdims=True))
            parts = [p.astype(BF16)]
            if r0 > 0:
                parts.insert(0, jnp.zeros((r0, 2 * gw), BF16))
            if r1 < band:
                parts.append(jnp.zeros((band - r1, 2 * gw), BF16))
            ps.append(jnp.concatenate(parts, axis=0))
        o_t = lax.dot_general(vb, jnp.concatenate(ps, axis=1), (((0,), (0,)), ((), ())),
                              preferred_element_type=F32)
        o = (o_t * jnp.concatenate(rden, axis=1)).T
        z = z_ref[pl.ds(q0, tq), :].astype(F32)
        o_ref[pl.ds(q0, tq), :] = (o * (z * jax.nn.sigmoid(z))).astype(o_ref.dtype)

    def run_blocks(blocks):
        s = scores(blocks[0])
        for j, blk in enumerate(blocks):
            s_next = scores(blocks[j + 1]) if j + 1 < len(blocks) else None
            finish(blk, s)
            s = s_next

    def interior(i):
        q0 = i * tq
        if not isinstance(i, int):
            q0 = pl.multiple_of(q0, tq)
        return 1, q0, q0 - (NA_WIN_H // 2) * gw

    u = NA_BLOCKS_PER_TRIP
    top = (0, 0, 0)
    bottom = (2, (nblk - 1) * tq, (rows - NA_BAND_ROWS) * gw)
    run_blocks([top] + [interior(i) for i in range(1, u)])

    def trip(t, carry):
        run_blocks([interior(t * u + j) for j in range(u)])
        return carry

    lax.fori_loop(1, nblk // u - 1, trip, 0)
    run_blocks([interior(i) for i in range(nblk - u, nblk - 1)] + [bottom])


def _neighbourhood_attention(qkvz, rpb):
    l = qkvz.shape[0]
    rows = l // GRID_W
    nblk = rows // NA_Q_ROWS
    assert rows % NA_Q_ROWS == 0 and nblk % NA_BLOCKS_PER_TRIP == 0 and nblk >= 2 * NA_BLOCKS_PER_TRIP
    nh = NA_HEADS

    def col(part):
        return pl.BlockSpec((l, NA_HEAD_DIM), lambda h: (0, part * nh + h))

    return pl.pallas_call(
        functools.partial(_na_kernel, rows=rows),
        grid=(nh,),
        in_specs=[pl.BlockSpec(memory_space=pltpu.SMEM), col(0), col(1), col(2), col(3)],
        out_specs=col(0),
        out_shape=jax.ShapeDtypeStruct((l, NA_D_INNER), BF16),
        scratch_shapes=[pltpu.VMEM((2 * NA_WIN_H - 1, GRID_W, 2 * GRID_W), F32),
                        pltpu.VMEM((3, NA_BAND_ROWS * GRID_W, NA_Q_ROWS * GRID_W), F32)],
        compiler_params=_cparams(1),
        name="neighbourhood_attention",
    )(rpb.reshape(-1), qkvz, qkvz, qkvz, qkvz)


def _ssm_layer(h, norm_w, w_in, conv_w, conv_b, dt_bias, a_log, d_skip, gnorm_w, w_out):
    n_zx = SSM_D_INNER + SSM_CONV_DIM

    hp = HEADS_PER_STEP
    heads = jnp.arange(SSM_HEADS).reshape(SSM_HEADS // hp, 1, hp)
    perm = (heads + jnp.array([0, SSM_HEADS]).reshape(1, 2, 1)).reshape(-1)
    w_dt = w_in[:, n_zx:][:, perm]
    bias = dt_bias.reshape(-1)[perm].reshape(1, -1)
    alog = a_log.reshape(-1)[perm].reshape(-1, 1)
    u, *dt_rows = _norm_dt_project(h, norm_w, w_dt, bias, alog)
    zx = _matmul(u, w_in, n_zx, BF16, tm=2048, tn=1024)
    nsteps = SSM_HEADS // hp
    shp = (nsteps, 2 * hp, h.shape[0])

    xbc = _conv_silu(zx, conv_w, conv_b, SSM_D_INNER)
    d_lanes = jnp.repeat(d_skip, SSM_HEAD_DIM).reshape(nsteps, 1, SSD_LANES)
    y = _ssd_scan(xbc, *[r.reshape(shp) for r in dt_rows], d_lanes)
    return _gated_out_proj(y, zx, gnorm_w, w_out, h)


def _na_layer(h, norm_w, w_in, rpb, w_out):
    u = _rmsnorm(h, norm_w, BF16)
    n = w_in.shape[1]
    qscale = NA_HEAD_DIM ** -0.5 * LOG2E
    colscale = jnp.where(jnp.arange(n) < NA_D_INNER, qscale, 1.0).astype(F32).reshape(1, n)
    qkvz = _matmul(u, w_in, n, BF16, tm=2048, tn=1024, colscale=colscale)
    o = _neighbourhood_attention(qkvz, rpb)
    return _matmul(o, w_out, w_out.shape[1], F32, tm=512, tn=1024, res=h, single_buffer_w=True)


def kernel(x, norm_w, ssm_w_in, ssm_conv_w, ssm_conv_b, ssm_dt_bias, ssm_A_log, ssm_D, ssm_norm_w,
           ssm_w_out, na_w_in, na_rpb, na_w_out, final_norm_w):
    b, l, d = x.shape
    outs = []
    for bi in range(b):
        h = x[bi]
        h = _ssm_layer(h, norm_w[0], ssm_w_in[0], ssm_conv_w[0], ssm_conv_b[0], ssm_dt_bias[0],
                       ssm_A_log[0], ssm_D[0], ssm_norm_w[0], ssm_w_out[0])
        h = _na_layer(h, norm_w[1], na_w_in[0], na_rpb[0], na_w_out[0])
        outs.append(_rmsnorm(h, final_norm_w, x.dtype))
    return jnp.stack(outs, axis=0)
```

```python
import functools

import jax
import jax.numpy as jnp
from jax import lax
from jax.experimental import pallas as pl
from jax.experimental.pallas import tpu as pltpu

F32 = jnp.float32
BF16 = jnp.bfloat16

GRID_W = 64
SSM_HEAD_DIM = 64
SSM_HEADS = 64
SSM_GROUPS = 8
SSM_STATE = 128
SSM_CONV_W = 7
SSM_CHUNK = 128
SSM_D_INNER = SSM_HEADS * SSM_HEAD_DIM
SSM_CONV_DIM = SSM_D_INNER + 2 * SSM_GROUPS * SSM_STATE
NA_HEAD_DIM = 128
NA_HEADS = 32
NA_D_INNER = NA_HEADS * NA_HEAD_DIM
NA_WIN_H = 8
NA_WIN_W = 16
NORM_EPS = 1e-5

HEADS_PER_STEP = 4
SSD_LANES = HEADS_PER_STEP * SSM_HEAD_DIM
SSD_CHUNKS_PER_TRIP = 16
NA_Q_ROWS = 4
NA_BAND_ROWS = NA_Q_ROWS + NA_WIN_H
NA_BLOCKS_PER_TRIP = 8
LOG2E = 1.4426950408889634
NEG_BIG = -1e30
VMEM_LIMIT = 56 * 1024 * 1024


def _cparams(n_axes):
    return pltpu.CompilerParams(dimension_semantics=("arbitrary",) * n_axes,
                                vmem_limit_bytes=VMEM_LIMIT)


def _rmsnorm_kernel(x_ref, w_ref, o_ref):
    x = x_ref[...]
    ms = jnp.mean(x * x, axis=-1, keepdims=True)
    o_ref[...] = (x * lax.rsqrt(ms + NORM_EPS) * w_ref[...]).astype(o_ref.dtype)


def _rmsnorm(x, w, out_dtype, tm=1024):
    l, d = x.shape
    tm = min(tm, l)
    return pl.pallas_call(
        _rmsnorm_kernel,
        grid=(l // tm,),
        in_specs=[pl.BlockSpec((tm, d), lambda i: (i, 0)),
                  pl.BlockSpec((1, d), lambda i: (0, 0))],
        out_specs=pl.BlockSpec((tm, d), lambda i: (i, 0)),
        out_shape=jax.ShapeDtypeStruct((l, d), out_dtype),
        compiler_params=_cparams(1),
        name="rmsnorm",
    )(x, w.reshape(1, d))


GATED_K_SLICES = 8


def _gated_out_proj_kernel(y_ref, z_ref, w_ref, g_ref, r_ref, o_ref, wbf_ref):
    k = y_ref.shape[1]
    ks = k // GATED_K_SLICES

    @pl.when(pl.program_id(1) == 0)
    def _():
        wbf_ref[...] = w_ref[...].astype(BF16)

    acc = None
    ss = None
    for i in range(GATED_K_SLICES):
        cols = slice(i * ks, (i + 1) * ks)
        z = z_ref[:, cols].astype(F32)
        g = y_ref[:, cols].astype(F32) * (z * jax.nn.sigmoid(z))
        part = jnp.sum(g * g, axis=-1, keepdims=True)
        prod = jnp.dot((g * g_ref[:, cols]).astype(BF16), wbf_ref[cols, :], preferred_element_type=F32)
        acc = prod if acc is None else acc + prod
        ss = part if ss is None else ss + part
    o_ref[...] = r_ref[...] + acc * lax.rsqrt(ss * (1.0 / k) + NORM_EPS)


def _gated_out_proj(y, zx, gain, w, res, tm=512, tn=1024):
    l, d = y.shape
    n = w.shape[1]
    tm = min(tm, l)
    return pl.pallas_call(
        _gated_out_proj_kernel,
        grid=(n // tn, l // tm),
        in_specs=[pl.BlockSpec((tm, d), lambda j, i: (i, 0)),
                  pl.BlockSpec((tm, d), lambda j, i: (i, 0)),
                  pl.BlockSpec((d, tn), lambda j, i: (0, j), pipeline_mode=pl.Buffered(1)),
                  pl.BlockSpec((1, d), lambda j, i: (0, 0)),
                  pl.BlockSpec((tm, tn), lambda j, i: (i, j))],
        out_specs=pl.BlockSpec((tm, tn), lambda j, i: (i, j)),
        out_shape=jax.ShapeDtypeStruct((l, n), F32),
        scratch_shapes=[pltpu.VMEM((d, tn), BF16)],
        compiler_params=_cparams(2),
        name="gated_out_proj",
    )(y, zx, w, gain.reshape(1, d), res)


def _matmul_kernel(*refs, has_scale, has_res):
    x_ref, w_ref = refs[0], refs[1]
    pos = 2
    s_ref = r_ref = None
    if has_scale:
        s_ref = refs[pos]
        pos += 1
    if has_res:
        r_ref = refs[pos]
        pos += 1
    o_ref, wbf_ref = refs[pos], refs[pos + 1]

    @pl.when(pl.program_id(1) == 0)
    def _():
        wbf_ref[...] = w_ref[...].astype(BF16)

    acc = jnp.dot(x_ref[...], wbf_ref[...], preferred_element_type=F32)
    if has_scale:
        acc = acc * s_ref[...]
    if has_res:
        acc = acc + r_ref[...]
    o_ref[...] = acc.astype(o_ref.dtype)


def _matmul(x, w, n_out, out_dtype, tm, tn, colscale=None, res=None, single_buffer_w=False):
    l, k = x.shape
    tm = min(tm, l)
    assert l % tm == 0 and n_out % tn == 0
    w_mode = dict(pipeline_mode=pl.Buffered(1)) if single_buffer_w else {}
    in_specs = [pl.BlockSpec((tm, k), lambda n, m: (m, 0)),
                pl.BlockSpec((k, tn), lambda n, m: (0, n), **w_mode)]
    args = [x, w]
    if colscale is not None:
        in_specs.append(pl.BlockSpec((1, tn), lambda n, m: (0, n)))
        args.append(colscale)
    if res is not None:
        in_specs.append(pl.BlockSpec((tm, tn), lambda n, m: (m, n)))
        args.append(res)
    return pl.pallas_call(
        functools.partial(_matmul_kernel, has_scale=colscale is not None, has_res=res is not None),
        grid=(n_out // tn, l // tm),
        in_specs=in_specs,
        out_specs=pl.BlockSpec((tm, tn), lambda n, m: (m, n)),
        out_shape=jax.ShapeDtypeStruct((l, n_out), out_dtype),
        scratch_shapes=[pltpu.VMEM((k, tn), BF16)],
        compiler_params=_cparams(2),
        name="matmul",
    )(*args)


def _split3_dot(a, m_bf16):
    hi = a.astype(BF16)
    r1 = a - hi.astype(F32)
    mid = r1.astype(BF16)
    lo = (r1 - mid.astype(F32)).astype(BF16)
    out = jnp.dot(hi, m_bf16, preferred_element_type=F32)
    out = out + jnp.dot(mid, m_bf16, preferred_element_type=F32)
    return out + jnp.dot(lo, m_bf16, preferred_element_type=F32)


def _dt_kernel(x_ref, g_ref, w_ref, bias_ref, alog_ref, u_ref, dt_ref, ac_ref, tot_ref, lg_ref):
    tm = x_ref.shape[0]
    x = x_ref[...]
    u = (x * lax.rsqrt(jnp.mean(x * x, axis=-1, keepdims=True) + NORM_EPS) * g_ref[...]).astype(BF16)
    u_ref[...] = u
    raw = jnp.dot(u, w_ref[...].astype(BF16), preferred_element_type=F32)
    xb = raw + bias_ref[...]
    dt = jnp.maximum(xb, 0.0) + jnp.log1p(jnp.exp(-jnp.abs(xb)))
    dt_t = dt.T
    a = dt_t * (-jnp.exp(alog_ref[...]) * LOG2E)
    q = SSM_CHUNK
    si = lax.broadcasted_iota(jnp.int32, (q, q), 0)
    ti = lax.broadcasted_iota(jnp.int32, (q, q), 1)
    upper = jnp.where(si <= ti, 1.0, 0.0).astype(BF16)
    row = lax.broadcasted_iota(jnp.int32, (a.shape[0], q), 0)
    is_bwd = (row % (2 * HEADS_PER_STEP)) >= HEADS_PER_STEP
    dt_ref[...] = dt_t
    log2_dt = jnp.log2(dt_t)
    for c in range(tm // q):
        cols = slice(c * q, (c + 1) * q)
        a_c = a[:, cols]
        inc = _split3_dot(a_c, upper)
        tot_ref[:, cols] = inc
        ac = jnp.where(is_bwd, inc - a_c, inc)
        ac_ref[:, cols] = ac
        lg_ref[:, cols] = log2_dt[:, cols] + jnp.where(is_bwd, ac, -ac)


def _norm_dt_project(x, gain, w_dt, bias, alog, tm=1024):
    l, k = x.shape
    n = w_dt.shape[1]
    tm = min(tm, l)
    out = jax.ShapeDtypeStruct((n, l), F32)
    return pl.pallas_call(
        _dt_kernel,
        grid=(l // tm,),
        in_specs=[pl.BlockSpec((tm, k), lambda i: (i, 0)),
                  pl.BlockSpec((1, k), lambda i: (0, 0)),
                  pl.BlockSpec((k, n), lambda i: (0, 0)),
                  pl.BlockSpec((1, n), lambda i: (0, 0)),
                  pl.BlockSpec((n, 1), lambda i: (0, 0))],
        out_specs=[pl.BlockSpec((tm, k), lambda i: (i, 0))] + [pl.BlockSpec((n, tm), lambda i: (0, i))] * 4,
        out_shape=[jax.ShapeDtypeStruct((l, k), BF16)] + [out] * 4,
        compiler_params=_cparams(1),
        name="norm_dt_project",
    )(x, gain.reshape(1, k), w_dt, bias, alog)


def _conv_silu_kernel(x_ref, w_ref, b_ref, o_ref, *, tile):
    l, cw = x_ref.shape
    pad = 16
    half = SSM_CONV_W // 2
    n_tiles = l // tile
    ext = tile + 2 * pad
    w = w_ref[...]
    b = b_ref[...]
    taps = [j for j in range(SSM_CONV_W) if j != half]
    ti = lax.broadcasted_iota(jnp.int32, (len(taps) * tile, ext), 0)
    ri = lax.broadcasted_iota(jnp.int32, (len(taps) * tile, ext), 1)
    tap_of_row = jnp.zeros_like(ti)
    for n, j in enumerate(taps):
        tap_of_row = jnp.where(ti // tile == n, j, tap_of_row)
    shift = jnp.where(ri == ti % tile + pad - half + tap_of_row, 1.0, 0.0).astype(BF16)

    def shifted_rows(i):
        r0 = pl.multiple_of(i * tile, tile)
        p0 = pl.multiple_of(jnp.maximum(r0 - pad, 0), pad)
        n0 = pl.multiple_of(jnp.minimum(r0 + tile, l - pad), pad)
        zero = jnp.zeros((pad, cw), BF16)
        prev = jnp.where(i > 0, x_ref[pl.ds(p0, pad), :], zero)
        nxt = jnp.where(i < n_tiles - 1, x_ref[pl.ds(n0, pad), :], zero)
        xe = jnp.concatenate([prev, x_ref[pl.ds(r0, tile), :], nxt], axis=0)
        return jnp.dot(shift, xe, preferred_element_type=F32)

    def body(i, carry):
        shifted = shifted_rows(i)
        r0 = pl.multiple_of(i * tile, tile)
        acc = b + x_ref[pl.ds(r0, tile), :].astype(F32) * w[half:half + 1, :]
        for n, j in enumerate(taps):
            acc = acc + shifted[n * tile:(n + 1) * tile, :] * w[j:j + 1, :]
        o_ref[pl.ds(r0, tile), :] = (acc * jax.nn.sigmoid(acc)).astype(o_ref.dtype)
        return carry

    lax.fori_loop(0, n_tiles, body, 0, unroll=min(32, n_tiles))


def _conv_silu(zx, conv_w, conv_b, col0, cw=256, tile=128):
    l = zx.shape[0]
    c = conv_w.shape[1]
    off = col0 // cw
    return pl.pallas_call(
        functools.partial(_conv_silu_kernel, tile=tile),
        grid=(c // cw,),
        in_specs=[pl.BlockSpec((l, cw), lambda j: (0, off + j)),
                  pl.BlockSpec((SSM_CONV_W, cw), lambda j: (0, j)),
                  pl.BlockSpec((1, cw), lambda j: (0, j))],
        out_specs=pl.BlockSpec((l, cw), lambda j: (0, j)),
        out_shape=jax.ShapeDtypeStruct((l, c), BF16),
        compiler_params=_cparams(1),
        name="conv_silu",
    )(zx, conv_w, conv_b.reshape(1, c))


def _head_selector(first_rows):
    n = len(first_rows)
    shape = (3 * 2 * HEADS_PER_STEP * n, SSD_LANES * n)
    row = lax.broadcasted_iota(jnp.int32, shape, 0)
    lane = lax.broadcasted_iota(jnp.int32, shape, 1)
    first = jnp.zeros(shape, jnp.int32)
    for i, f in enumerate(first_rows):
        first = jnp.where(lane // SSD_LANES == i, f, first)
    hit = (row // (3 * 2 * HEADS_PER_STEP) == lane // SSD_LANES) & (
        row % (2 * HEADS_PER_STEP) == (lane % SSD_LANES) // SSM_HEAD_DIM + first)
    return jnp.where(hit, 1.0, 0.0).astype(BF16)


def _rows_to_lanes(blocks, selector):
    pieces = []
    for r8 in blocks:
        hi = r8.astype(BF16).astype(F32)
        r1 = r8 - hi
        mid = r1.astype(BF16).astype(F32)
        pieces += [hi, mid, r1 - mid]
    lhs = jnp.concatenate(pieces, axis=0).astype(BF16)
    return lax.dot_general(lhs, selector, (((0,), (0,)), ((), ())), preferred_element_type=F32)


def _head_lane_vector(c4, lanehead):
    out = jnp.broadcast_to(c4[0:1, :], (1, SSD_LANES))
    for k in range(1, HEADS_PER_STEP):
        out = jnp.where(lanehead == k, jnp.broadcast_to(c4[k:k + 1, :], (1, SSD_LANES)), out)
    return out


def _ssd_kernel(x_ref, b_ref, c_ref, dt_ref, ac_ref, tot_ref, lg_ref, d_ref, y_ref, hf_ref, h_ref):
    l = x_ref.shape[0]
    q = SSM_CHUNK
    nc = l // q
    hp = HEADS_PER_STEP
    lanehead = lax.broadcasted_iota(jnp.int32, (1, SSD_LANES), 1) // SSM_HEAD_DIM
    ri = lax.broadcasted_iota(jnp.int32, (q, q), 0)
    ci = lax.broadcasted_iota(jnp.int32, (q, q), 1)
    contract0 = (((0,), (0,)), ((), ()))
    contract_last = (((1,), (1,)), ((), ()))
    sel_f = _head_selector([0])
    sel_fbb = _head_selector([0, hp, hp])
    u = SSD_CHUNKS_PER_TRIP

    def chunk_rows(c):
        return pl.ds(pl.multiple_of(c * q, q), q)


    h_ref[...] = jnp.zeros_like(h_ref)

    def fwd_trip(t, carry):
        chunks = [t * u + j for j in range(u)]
        a_lasts, wfs, ss = [], [], []
        for c in chunks:
            rows = chunk_rows(c)
            acr = ac_ref[0, :, rows]
            a_lasts.append(acr[:, q - 1:q])
            wfs.append(_rows_to_lanes([jnp.exp2(a_lasts[-1] - acr) * dt_ref[0, :, rows]], sel_f))
        for c, wf in zip(chunks, wfs):
            rows = chunk_rows(c)
            xw = (x_ref[rows, :].astype(F32) * wf).astype(BF16)
            ss.append(lax.dot_general(b_ref[rows, :], xw, contract0, preferred_element_type=F32))
        h = h_ref[...]
        for c, a_last, s in zip(chunks, a_lasts, ss):
            hf_ref[c] = h.astype(BF16)
            h = h * _head_lane_vector(jnp.exp2(a_last[0:hp]), lanehead) + s
        h_ref[...] = h
        return carry

    lax.fori_loop(0, nc // u, fwd_trip, 0)

    h_ref[...] = jnp.zeros_like(h_ref)

    def stage_a(c):
        rows = chunk_rows(c)
        bc, cc = b_ref[rows, :], c_ref[rows, :]
        dtr, acr = dt_ref[0, :, rows], ac_ref[0, :, rows]
        t_last = tot_ref[0, :, rows][:, q - 1:q]
        g = lax.dot_general(cc, bc, contract_last, preferred_element_type=F32)
        scales = _rows_to_lanes([jnp.exp2(acr), jnp.exp2(t_last - acr), jnp.exp2(acr) * dtr], sel_fbb)
        yoff_f = jnp.dot(cc, hf_ref[c], preferred_element_type=F32) * scales[:, 0:SSD_LANES]
        return dict(rows=rows, bc=bc, cc=cc, dtr=dtr, acr=acr, t_last=t_last, g=g, yoff_f=yoff_f,
                    scale_b=scales[:, SSD_LANES:2 * SSD_LANES], wb=scales[:, 2 * SSD_LANES:])

    def stage_b(d):
        xw = (x_ref[d["rows"], :].astype(F32) * d["wb"]).astype(BF16)
        d["s"] = lax.dot_general(d["bc"], xw, contract0, preferred_element_type=F32)

    def stage_c(d, hb):
        chb = jnp.dot(d["cc"], hb.astype(BF16), preferred_element_type=F32)
        hb = hb * _head_lane_vector(jnp.exp2(d["t_last"][hp:2 * hp]), lanehead) + d["s"]
        dtr, acr, g = d["dtr"], d["acr"], d["g"]
        lgr = lg_ref[0, :, d["rows"]]
        ncb = -acr[hp:2 * hp]
        dsum = dtr[0:hp] + dtr[hp:2 * hp]
        ms = []
        for k in range(hp):
            col_part = jnp.where(ri <= ci, acr[k:k + 1, :], ncb[k:k + 1, :]).T
            row_part = jnp.where(ci <= ri, lgr[k:k + 1, :], lgr[hp + k:hp + k + 1, :])
            e = jnp.where(ci == ri, dsum[k:k + 1, :], jnp.exp2(col_part + row_part))
            ms.append((g * e).astype(BF16))
        mcat = jnp.concatenate(ms, axis=1)
        xc = x_ref[d["rows"], :]
        xstack = jnp.concatenate(
            [jnp.where(lanehead == k, xc, jnp.zeros_like(xc)) for k in range(hp)], axis=0)
        y = jnp.dot(mcat, xstack, preferred_element_type=F32)
        y = y + d["yoff_f"] + chb * d["scale_b"] + d_ref[0] * xc.astype(F32)
        y_ref[d["rows"], :] = y.astype(y_ref.dtype)
        return hb

    def bwd_trip(t, carry):
        chunks = [nc - 1 - (t * u + j) for j in range(u)]
        st = [None] * u
        st[0] = stage_a(chunks[0])
        if u > 1:
            st[1] = stage_a(chunks[1])
        stage_b(st[0])
        hb = h_ref[...]
        for j in range(u):
            if j + 2 < u:
                st[j + 2] = stage_a(chunks[j + 2])
            if j + 1 < u:
                stage_b(st[j + 1])
            hb = stage_c(st[j], hb)
            st[j] = None
        h_ref[...] = hb
        return carry

    lax.fori_loop(0, nc // u, bwd_trip, 0)


def _ssd_scan(xbc, dt_t, ac_t, tot_t, lg_t, d_lanes):
    l = xbc.shape[0]
    assert l % (SSM_CHUNK * SSD_CHUNKS_PER_TRIP) == 0
    nsteps = SSM_HEADS // HEADS_PER_STEP
    per_group = SSM_HEADS // SSM_GROUPS // HEADS_PER_STEP
    b_off = SSM_D_INNER // SSM_STATE
    c_off = b_off + SSM_GROUPS
    small = pl.BlockSpec((1, 2 * HEADS_PER_STEP, l), lambda j: (j, 0, 0))
    return pl.pallas_call(
        _ssd_kernel,
        grid=(nsteps,),
        in_specs=[pl.BlockSpec((l, SSD_LANES), lambda j: (0, j)),
                  pl.BlockSpec((l, SSM_STATE), lambda j: (0, b_off + j // per_group)),
                  pl.BlockSpec((l, SSM_STATE), lambda j: (0, c_off + j // per_group)),
                  small, small, small, small,
                  pl.BlockSpec((1, 1, SSD_LANES), lambda j: (j, 0, 0))],
        out_specs=pl.BlockSpec((l, SSD_LANES), lambda j: (0, j)),
        out_shape=jax.ShapeDtypeStruct((l, SSM_D_INNER), BF16),
        scratch_shapes=[pltpu.VMEM((l // SSM_CHUNK, SSM_STATE, SSD_LANES), BF16),
                        pltpu.VMEM((SSM_STATE, SSD_LANES), F32)],
        compiler_params=_cparams(1),
        name="ssd_scan",
    )(xbc, xbc, xbc, dt_t, ac_t, tot_t, lg_t, d_lanes)


def _na_window(case, qr):
    if case == 0:
        return 0, NA_WIN_H, -qr
    if case == 1:
        return qr, qr + NA_WIN_H, -(NA_WIN_H // 2) - qr
    return NA_Q_ROWS, NA_BAND_ROWS, -NA_WIN_H - qr


def _na_kernel(rpb_ref, q_ref, k_ref, v_ref, z_ref, o_ref, t_ref, bias_ref, *, rows):
    h = pl.program_id(0)
    nblk = rows // NA_Q_ROWS
    gw = GRID_W
    tq = NA_Q_ROWS * gw
    band = NA_BAND_ROWS * gw
    n_dr = 2 * NA_WIN_H - 1
    n_dc = 2 * NA_WIN_W - 1

    kc = lax.broadcasted_iota(jnp.int32, (gw, 2 * gw), 0)
    lane = lax.broadcasted_iota(jnp.int32, (gw, 2 * gw), 1)
    qc = lane % gw
    diff = kc - qc + (NA_WIN_W - 1)
    cs = jnp.clip(qc - NA_WIN_W // 2, 0, gw - NA_WIN_W)
    col_ok = (kc >= cs) & (kc < cs + NA_WIN_W)
    tables = [jnp.full((gw, 2 * gw), NEG_BIG, F32)] * n_dr
    for j in range(n_dc):
        on_diagonal = (diff == j) & col_ok
        for dr in range(n_dr):
            tables[dr] = jnp.where(on_diagonal, rpb_ref[(h * n_dr + dr) * n_dc + j] * LOG2E, tables[dr])
    for dr in range(n_dr):
        t_ref[dr] = tables[dr]
    neg = jnp.full((gw, 2 * gw), NEG_BIG, F32)
    for case in range(3):
        for kr in range(NA_BAND_ROWS):
            for pair in range(NA_Q_ROWS // 2):
                halves = []
                for qr in (2 * pair, 2 * pair + 1):
                    lo, hi, off = _na_window(case, qr)
                    halves.append(t_ref[kr + off + NA_WIN_H - 1] if lo <= kr < hi else neg)
                bias_ref[case, kr * gw:(kr + 1) * gw, pair * 2 * gw:(pair + 1) * 2 * gw] = (
                    jnp.where(lane < gw, halves[0], halves[1]))

    def scores(blk):
        _, q0, k0 = blk
        return lax.dot_general(k_ref[pl.ds(k0, band), :], q_ref[pl.ds(q0, tq), :],
                               (((1,), (1,)), ((), ())), preferred_element_type=F32)

    def finish(blk, s):
        case, q0, k0 = blk
        vb = v_ref[pl.ds(k0, band), :]
        ps, rden = [], []
        for pair in range(NA_Q_ROWS // 2):
            wins = [_na_window(case, qr) for qr in (2 * pair, 2 * pair + 1)]
            r0, r1 = min(w[0] for w in wins) * gw, max(w[1] for w in wins) * gw
            lanes = slice(pair * 2 * gw, (pair + 1) * 2 * gw)
            sq = s[r0:r1, lanes] + bias_ref[case, r0:r1, lanes]
            m = jnp.max(sq, axis=0, keepdims=True)
            p = jnp.exp2(sq - m)
            rden.append(1.0 / jnp.sum(p, axis=0, keepdims=True))
            parts = [p.astype(BF16)]
            if r0 > 0:
                parts.insert(0, jnp.zeros((r0, 2 * gw), BF16))
            if r1 < band:
                parts.append(jnp.zeros((band - r1, 2 * gw), BF16))
            ps.append(jnp.concatenate(parts, axis=0))
        o_t = lax.dot_general(vb, jnp.concatenate(ps, axis=1), (((0,), (0,)), ((), ())),
                              preferred_element_type=F32)
        o = (o_t * jnp.concatenate(rden, axis=1)).T
        z = z_ref[pl.ds(q0, tq), :].astype(F32)
        o_ref[pl.ds(q0, tq), :] = (o * (z * jax.nn.sigmoid(z))).astype(o_ref.dtype)

    def run_blocks(blocks):
        s = scores(blocks[0])
        for j, blk in enumerate(blocks):
            s_next = scores(blocks[j + 1]) if j + 1 < len(blocks) else None
            finish(blk, s)
            s = s_next

    def interior(i):
        q0 = i * tq
        if not isinstance(i, int):
            q0 = pl.multiple_of(q0, tq)
        return 1, q0, q0 - (NA_WIN_H // 2) * gw

    u = NA_BLOCKS_PER_TRIP
    top = (0, 0, 0)
    bottom = (2, (nblk - 1) * tq, (rows - NA_BAND_ROWS) * gw)
    run_blocks([top] + [interior(i) for i in range(1, u)])

    def trip(t, carry):
        run_blocks([interior(t * u + j) for j in range(u)])
        return carry

    lax.fori_loop(1, nblk // u - 1, trip, 0)
    run_blocks([interior(i) for i in range(nblk - u, nblk - 1)] + [bottom])


def _neighbourhood_attention(qkvz, rpb):
    l = qkvz.shape[0]
    rows = l // GRID_W
    nblk = rows // NA_Q_ROWS
    assert rows % NA_Q_ROWS == 0 and nblk % NA_BLOCKS_PER_TRIP == 0 and nblk >= 2 * NA_BLOCKS_PER_TRIP
    nh = NA_HEADS

    def col(part):
        return pl.BlockSpec((l, NA_HEAD_DIM), lambda h: (0, part * nh + h))

    return pl.pallas_call(
        functools.partial(_na_kernel, rows=rows),
        grid=(nh,),
        in_specs=[pl.BlockSpec(memory_space=pltpu.SMEM), col(0), col(1), col(2), col(3)],
        out_specs=col(0),
        out_shape=jax.ShapeDtypeStruct((l, NA_D_INNER), BF16),
        scratch_shapes=[pltpu.VMEM((2 * NA_WIN_H - 1, GRID_W, 2 * GRID_W), F32),
                        pltpu.VMEM((3, NA_BAND_ROWS * GRID_W, NA_Q_ROWS * GRID_W), F32)],
        compiler_params=_cparams(1),
        name="neighbourhood_attention",
    )(rpb.reshape(-1), qkvz, qkvz, qkvz, qkvz)


def _ssm_layer(h, norm_w, w_in, conv_w, conv_b, dt_bias, a_log, d_skip, gnorm_w, w_out):
    n_zx = SSM_D_INNER + SSM_CONV_DIM

    hp = HEADS_PER_STEP
    heads = jnp.arange(SSM_HEADS).reshape(SSM_HEADS // hp, 1, hp)
    perm = (heads + jnp.array([0, SSM_HEADS]).reshape(1, 2, 1)).reshape(-1)
    w_dt = w_in[:, n_zx:][:, perm]
    bias = dt_bias.reshape(-1)[perm].reshape(1, -1)
    alog = a_log.reshape(-1)[perm].reshape(-1, 1)
    u, *dt_rows = _norm_dt_project(h, norm_w, w_dt, bias, alog)
    zx = _matmul(u, w_in, n_zx, BF16, tm=2048, tn=1024)
    nsteps = SSM_HEADS // hp
    shp = (nsteps, 2 * hp, h.shape[0])

    xbc = _conv_silu(zx, conv_w, conv_b, SSM_D_INNER)
    d_lanes = jnp.repeat(d_skip, SSM_HEAD_DIM).reshape(nsteps, 1, SSD_LANES)
    y = _ssd_scan(xbc, *[r.reshape(shp) for r in dt_rows], d_lanes)
    return _gated_out_proj(y, zx, gnorm_w, w_out, h)


def _na_layer(h, norm_w, w_in, rpb, w_out):
    u = _rmsnorm(h, norm_w, BF16)
    n = w_in.shape[1]
    qscale = NA_HEAD_DIM ** -0.5 * LOG2E
    colscale = jnp.where(jnp.arange(n) < NA_D_INNER, qscale, 1.0).astype(F32).reshape(1, n)
    qkvz = _matmul(u, w_in, n, BF16, tm=2048, tn=1024, colscale=colscale)
    o = _neighbourhood_attention(qkvz, rpb)
    return _matmul(o, w_out, w_out.shape[1], F32, tm=512, tn=1024, res=h, single_buffer_w=True)


def kernel(x, norm_w, ssm_w_in, ssm_conv_w, ssm_conv_b, ssm_dt_bias, ssm_A_log, ssm_D, ssm_norm_w,
           ssm_w_out, na_w_in, na_rpb, na_w_out, final_norm_w):
    b, l, d = x.shape
    outs = []
    for bi in range(b):
        h = x[bi]
        h = _ssm_layer(h, norm_w[0], ssm_w_in[0], ssm_conv_w[0], ssm_conv_b[0], ssm_dt_bias[0],
                       ssm_A_log[0], ssm_D[0], ssm_norm_w[0], ssm_w_out[0])
        h = _na_layer(h, norm_w[1], na_w_in[0], na_rpb[0], na_w_out[0])
        outs.append(_rmsnorm(h, final_norm_w, x.dtype))
    return jnp.stack(outs, axis=0)
```

```python
import functools

import jax
import jax.numpy as jnp
from jax import lax
from jax.experimental import pallas as pl
from jax.experimental.pallas import tpu as pltpu

F32 = jnp.float32
BF16 = jnp.bfloat16

GRID_W = 64
SSM_HEAD_DIM = 64
SSM_HEADS = 64
SSM_GROUPS = 8
SSM_STATE = 128
SSM_CONV_W = 7
SSM_CHUNK = 128
SSM_D_INNER = SSM_HEADS * SSM_HEAD_DIM
SSM_CONV_DIM = SSM_D_INNER + 2 * SSM_GROUPS * SSM_STATE
NA_HEAD_DIM = 128
NA_HEADS = 32
NA_D_INNER = NA_HEADS * NA_HEAD_DIM
NA_WIN_H = 8
NA_WIN_W = 16
NORM_EPS = 1e-5

HEADS_PER_STEP = 4
SSD_LANES = HEADS_PER_STEP * SSM_HEAD_DIM
SSD_CHUNKS_PER_TRIP = 32
NA_Q_ROWS = 4
NA_BAND_ROWS = NA_Q_ROWS + NA_WIN_H
NA_BLOCKS_PER_TRIP = 16
LOG2E = 1.4426950408889634
NEG_BIG = -1e30
VMEM_LIMIT = 56 * 1024 * 1024


def _cparams(n_axes):
    return pltpu.CompilerParams(dimension_semantics=("arbitrary",) * n_axes,
                                vmem_limit_bytes=VMEM_LIMIT)


def _rmsnorm_kernel(x_ref, w_ref, o_ref):
    x = x_ref[...]
    ms = jnp.mean(x * x, axis=-1, keepdims=True)
    o_ref[...] = (x * lax.rsqrt(ms + NORM_EPS) * w_ref[...]).astype(o_ref.dtype)


def _rmsnorm(x, w, out_dtype, tm=1024):
    l, d = x.shape
    tm = min(tm, l)
    return pl.pallas_call(
        _rmsnorm_kernel,
        grid=(l // tm,),
        in_specs=[pl.BlockSpec((tm, d), lambda i: (i, 0)),
                  pl.BlockSpec((1, d), lambda i: (0, 0))],
        out_specs=pl.BlockSpec((tm, d), lambda i: (i, 0)),
        out_shape=jax.ShapeDtypeStruct((l, d), out_dtype),
        compiler_params=_cparams(1),
        name="rmsnorm",
    )(x, w.reshape(1, d))


GATED_K_SLICES = 8


def _gated_out_proj_kernel(y_ref, z_ref, w_ref, g_ref, r_ref, o_ref, wbf_ref):
    k = y_ref.shape[1]
    ks = k // GATED_K_SLICES

    @pl.when(pl.program_id(1) == 0)
    def _():
        wbf_ref[...] = w_ref[...].astype(BF16)

    acc = None
    ss = None
    for i in range(GATED_K_SLICES):
        cols = slice(i * ks, (i + 1) * ks)
        z = z_ref[:, cols].astype(F32)
        g = y_ref[:, cols].astype(F32) * (z * jax.nn.sigmoid(z))
        part = jnp.sum(g * g, axis=-1, keepdims=True)
        prod = jnp.dot((g * g_ref[:, cols]).astype(BF16), wbf_ref[cols, :], preferred_element_type=F32)
        acc = prod if acc is None else acc + prod
        ss = part if ss is None else ss + part
    o_ref[...] = r_ref[...] + acc * lax.rsqrt(ss * (1.0 / k) + NORM_EPS)


def _gated_out_proj(y, zx, gain, w, res, tm=512, tn=1024):
    l, d = y.shape
    n = w.shape[1]
    tm = min(tm, l)
    return pl.pallas_call(
        _gated_out_proj_kernel,
        grid=(n // tn, l // tm),
        in_specs=[pl.BlockSpec((tm, d), lambda j, i: (i, 0)),
                  pl.BlockSpec((tm, d), lambda j, i: (i, 0)),
                  pl.BlockSpec((d, tn), lambda j, i: (0, j), pipeline_mode=pl.Buffered(1)),
                  pl.BlockSpec((1, d), lambda j, i: (0, 0)),
                  pl.BlockSpec((tm, tn), lambda j, i: (i, j))],
        out_specs=pl.BlockSpec((tm, tn), lambda j, i: (i, j)),
        out_shape=jax.ShapeDtypeStruct((l, n), F32),
        scratch_shapes=[pltpu.VMEM((d, tn), BF16)],
        compiler_params=_cparams(2),
        name="gated_out_proj",
    )(y, zx, w, gain.reshape(1, d), res)


def _matmul_kernel(*refs, has_scale, has_res):
    x_ref, w_ref = refs[0], refs[1]
    pos = 2
    s_ref = r_ref = None
    if has_scale:
        s_ref = refs[pos]
        pos += 1
    if has_res:
        r_ref = refs[pos]
        pos += 1
    o_ref, wbf_ref = refs[pos], refs[pos + 1]

    @pl.when(pl.program_id(1) == 0)
    def _():
        wbf_ref[...] = w_ref[...].astype(BF16)

    acc = jnp.dot(x_ref[...], wbf_ref[...], preferred_element_type=F32)
    if has_scale:
        acc = acc * s_ref[...]
    if has_res:
        acc = acc + r_ref[...]
    o_ref[...] = acc.astype(o_ref.dtype)


def _matmul(x, w, n_out, out_dtype, tm, tn, colscale=None, res=None, single_buffer_w=False):
    l, k = x.shape
    tm = min(tm, l)
    assert l % tm == 0 and n_out % tn == 0
    w_mode = dict(pipeline_mode=pl.Buffered(1)) if single_buffer_w else {}
    in_specs = [pl.BlockSpec((tm, k), lambda n, m: (m, 0)),
                pl.BlockSpec((k, tn), lambda n, m: (0, n), **w_mode)]
    args = [x, w]
    if colscale is not None:
        in_specs.append(pl.BlockSpec((1, tn), lambda n, m: (0, n)))
        args.append(colscale)
    if res is not None:
        in_specs.append(pl.BlockSpec((tm, tn), lambda n, m: (m, n)))
        args.append(res)
    return pl.pallas_call(
        functools.partial(_matmul_kernel, has_scale=colscale is not None, has_res=res is not None),
        grid=(n_out // tn, l // tm),
        in_specs=in_specs,
        out_specs=pl.BlockSpec((tm, tn), lambda n, m: (m, n)),
        out_shape=jax.ShapeDtypeStruct((l, n_out), out_dtype),
        scratch_shapes=[pltpu.VMEM((k, tn), BF16)],
        compiler_params=_cparams(2),
        name="matmul",
    )(*args)


def _split3_dot(a, m_bf16):
    hi = a.astype(BF16)
    r1 = a - hi.astype(F32)
    mid = r1.astype(BF16)
    lo = (r1 - mid.astype(F32)).astype(BF16)
    out = jnp.dot(hi, m_bf16, preferred_element_type=F32)
    out = out + jnp.dot(mid, m_bf16, preferred_element_type=F32)
    return out + jnp.dot(lo, m_bf16, preferred_element_type=F32)


def _dt_kernel(x_ref, g_ref, w_ref, bias_ref, alog_ref, u_ref, dt_ref, ac_ref, tot_ref, lg_ref):
    tm = x_ref.shape[0]
    x = x_ref[...]
    u = (x * lax.rsqrt(jnp.mean(x * x, axis=-1, keepdims=True) + NORM_EPS) * g_ref[...]).astype(BF16)
    u_ref[...] = u
    raw = jnp.dot(u, w_ref[...].astype(BF16), preferred_element_type=F32)
    xb = raw + bias_ref[...]
    dt = jnp.maximum(xb, 0.0) + jnp.log1p(jnp.exp(-jnp.abs(xb)))
    dt_t = dt.T
    a = dt_t * (-jnp.exp(alog_ref[...]) * LOG2E)
    q = SSM_CHUNK
    si = lax.broadcasted_iota(jnp.int32, (q, q), 0)
    ti = lax.broadcasted_iota(jnp.int32, (q, q), 1)
    upper = jnp.where(si <= ti, 1.0, 0.0).astype(BF16)
    row = lax.broadcasted_iota(jnp.int32, (a.shape[0], q), 0)
    is_bwd = (row % (2 * HEADS_PER_STEP)) >= HEADS_PER_STEP
    dt_ref[...] = dt_t
    log2_dt = jnp.log2(dt_t)
    for c in range(tm // q):
        cols = slice(c * q, (c + 1) * q)
        a_c = a[:, cols]
        inc = _split3_dot(a_c, upper)
        tot_ref[:, cols] = inc
        ac = jnp.where(is_bwd, inc - a_c, inc)
        ac_ref[:, cols] = ac
        lg_ref[:, cols] = log2_dt[:, cols] + jnp.where(is_bwd, ac, -ac)


def _norm_dt_project(x, gain, w_dt, bias, alog, tm=1024):
    l, k = x.shape
    n = w_dt.shape[1]
    tm = min(tm, l)
    out = jax.ShapeDtypeStruct((n, l), F32)
    return pl.pallas_call(
        _dt_kernel,
        grid=(l // tm,),
        in_specs=[pl.BlockSpec((tm, k), lambda i: (i, 0)),
                  pl.BlockSpec((1, k), lambda i: (0, 0)),
                  pl.BlockSpec((k, n), lambda i: (0, 0)),
                  pl.BlockSpec((1, n), lambda i: (0, 0)),
                  pl.BlockSpec((n, 1), lambda i: (0, 0))],
        out_specs=[pl.BlockSpec((tm, k), lambda i: (i, 0))] + [pl.BlockSpec((n, tm), lambda i: (0, i))] * 4,
        out_shape=[jax.ShapeDtypeStruct((l, k), BF16)] + [out] * 4,
        compiler_params=_cparams(1),
        name="norm_dt_project",
    )(x, gain.reshape(1, k), w_dt, bias, alog)


def _conv_silu_kernel(x_ref, w_ref, b_ref, o_ref, *, tile):
    l, cw = x_ref.shape
    pad = 16
    half = SSM_CONV_W // 2
    n_tiles = l // tile
    ext = tile + 2 * pad
    w = w_ref[...]
    b = b_ref[...]
    taps = [j for j in range(SSM_CONV_W) if j != half]
    ti = lax.broadcasted_iota(jnp.int32, (len(taps) * tile, ext), 0)
    ri = lax.broadcasted_iota(jnp.int32, (len(taps) * tile, ext), 1)
    tap_of_row = jnp.zeros_like(ti)
    for n, j in enumerate(taps):
        tap_of_row = jnp.where(ti // tile == n, j, tap_of_row)
    shift = jnp.where(ri == ti % tile + pad - half + tap_of_row, 1.0, 0.0).astype(BF16)

    def shifted_rows(i):
        r0 = pl.multiple_of(i * tile, tile)
        p0 = pl.multiple_of(jnp.maximum(r0 - pad, 0), pad)
        n0 = pl.multiple_of(jnp.minimum(r0 + tile, l - pad), pad)
        zero = jnp.zeros((pad, cw), BF16)
        prev = jnp.where(i > 0, x_ref[pl.ds(p0, pad), :], zero)
        nxt = jnp.where(i < n_tiles - 1, x_ref[pl.ds(n0, pad), :], zero)
        xe = jnp.concatenate([prev, x_ref[pl.ds(r0, tile), :], nxt], axis=0)
        return jnp.dot(shift, xe, preferred_element_type=F32)

    def body(i, carry):
        shifted = shifted_rows(i)
        r0 = pl.multiple_of(i * tile, tile)
        acc = b + x_ref[pl.ds(r0, tile), :].astype(F32) * w[half:half + 1, :]
        for n, j in enumerate(taps):
            acc = acc + shifted[n * tile:(n + 1) * tile, :] * w[j:j + 1, :]
        o_ref[pl.ds(r0, tile), :] = (acc * jax.nn.sigmoid(acc)).astype(o_ref.dtype)
        return carry

    lax.fori_loop(0, n_tiles, body, 0, unroll=min(32, n_tiles))


def _conv_silu(zx, conv_w, conv_b, col0, cw=256, tile=128):
    l = zx.shape[0]
    c = conv_w.shape[1]
    off = col0 // cw
    return pl.pallas_call(
        functools.partial(_conv_silu_kernel, tile=tile),
        grid=(c // cw,),
        in_specs=[pl.BlockSpec((l, cw), lambda j: (0, off + j)),
                  pl.BlockSpec((SSM_CONV_W, cw), lambda j: (0, j)),
                  pl.BlockSpec((1, cw), lambda j: (0, j))],
        out_specs=pl.BlockSpec((l, cw), lambda j: (0, j)),
        out_shape=jax.ShapeDtypeStruct((l, c), BF16),
        compiler_params=_cparams(1),
        name="conv_silu",
    )(zx, conv_w, conv_b.reshape(1, c))


def _head_selector(first_rows):
    n = len(first_rows)
    shape = (3 * 2 * HEADS_PER_STEP * n, SSD_LANES * n)
    row = lax.broadcasted_iota(jnp.int32, shape, 0)
    lane = lax.broadcasted_iota(jnp.int32, shape, 1)
    first = jnp.zeros(shape, jnp.int32)
    for i, f in enumerate(first_rows):
        first = jnp.where(lane // SSD_LANES == i, f, first)
    hit = (row // (3 * 2 * HEADS_PER_STEP) == lane // SSD_LANES) & (
        row % (2 * HEADS_PER_STEP) == (lane % SSD_LANES) // SSM_HEAD_DIM + first)
    return jnp.where(hit, 1.0, 0.0).astype(BF16)


def _rows_to_lanes(blocks, selector):
    pieces = []
    for r8 in blocks:
        hi = r8.astype(BF16).astype(F32)
        r1 = r8 - hi
        mid = r1.astype(BF16).astype(F32)
        pieces += [hi, mid, r1 - mid]
    lhs = jnp.concatenate(pieces, axis=0).astype(BF16)
    return lax.dot_general(lhs, selector, (((0,), (0,)), ((), ())), preferred_element_type=F32)


def _head_lane_vector(c4, lanehead):
    out = jnp.broadcast_to(c4[0:1, :], (1, SSD_LANES))
    for k in range(1, HEADS_PER_STEP):
        out = jnp.where(lanehead == k, jnp.broadcast_to(c4[k:k + 1, :], (1, SSD_LANES)), out)
    return out


def _ssd_kernel(x_ref, b_ref, c_ref, dt_ref, ac_ref, tot_ref, lg_ref, d_ref, y_ref, hf_ref, h_ref):
    l = x_ref.shape[0]
    q = SSM_CHUNK
    nc = l // q
    hp = HEADS_PER_STEP
    lanehead = lax.broadcasted_iota(jnp.int32, (1, SSD_LANES), 1) // SSM_HEAD_DIM
    ri = lax.broadcasted_iota(jnp.int32, (q, q), 0)
    ci = lax.broadcasted_iota(jnp.int32, (q, q), 1)
    contract0 = (((0,), (0,)), ((), ()))
    contract_last = (((1,), (1,)), ((), ()))
    sel_f = _head_selector([0])
    sel_fbb = _head_selector([0, hp, hp])
    u = SSD_CHUNKS_PER_TRIP

    def chunk_rows(c):
        return pl.ds(pl.multiple_of(c * q, q), q)


    h_ref[...] = jnp.zeros_like(h_ref)

    def fwd_trip(t, carry):
        chunks = [t * u + j for j in range(u)]
        a_lasts, wfs, ss = [], [], []
        for c in chunks:
            rows = chunk_rows(c)
            acr = ac_ref[0, :, rows]
            a_lasts.append(acr[:, q - 1:q])
            wfs.append(_rows_to_lanes([jnp.exp2(a_lasts[-1] - acr) * dt_ref[0, :, rows]], sel_f))
        for c, wf in zip(chunks, wfs):
            rows = chunk_rows(c)
            xw = (x_ref[rows, :].astype(F32) * wf).astype(BF16)
            ss.append(lax.dot_general(b_ref[rows, :], xw, contract0, preferred_element_type=F32))
        h = h_ref[...]
        for c, a_last, s in zip(chunks, a_lasts, ss):
            hf_ref[c] = h.astype(BF16)
            h = h * _head_lane_vector(jnp.exp2(a_last[0:hp]), lanehead) + s
        h_ref[...] = h
        return carry

    lax.fori_loop(0, nc // u, fwd_trip, 0)

    h_ref[...] = jnp.zeros_like(h_ref)

    def stage_a(c):
        rows = chunk_rows(c)
        bc, cc = b_ref[rows, :], c_ref[rows, :]
        dtr, acr = dt_ref[0, :, rows], ac_ref[0, :, rows]
        t_last = tot_ref[0, :, rows][:, q - 1:q]
        g = lax.dot_general(cc, bc, contract_last, preferred_element_type=F32)
        scales = _rows_to_lanes([jnp.exp2(acr), jnp.exp2(t_last - acr), jnp.exp2(acr) * dtr], sel_fbb)
        yoff_f = jnp.dot(cc, hf_ref[c], preferred_element_type=F32) * scales[:, 0:SSD_LANES]
        return dict(rows=rows, bc=bc, cc=cc, dtr=dtr, acr=acr, t_last=t_last, g=g, yoff_f=yoff_f,
                    scale_b=scales[:, SSD_LANES:2 * SSD_LANES], wb=scales[:, 2 * SSD_LANES:])

    def stage_b(d):
        xw = (x_ref[d["rows"], :].astype(F32) * d["wb"]).astype(BF16)
        d["s"] = lax.dot_general(d["bc"], xw, contract0, preferred_element_type=F32)

    def stage_c(d, hb):
        chb = jnp.dot(d["cc"], hb.astype(BF16), preferred_element_type=F32)
        hb = hb * _head_lane_vector(jnp.exp2(d["t_last"][hp:2 * hp]), lanehead) + d["s"]
        dtr, acr, g = d["dtr"], d["acr"], d["g"]
        lgr = lg_ref[0, :, d["rows"]]
        ncb = -acr[hp:2 * hp]
        dsum = dtr[0:hp] + dtr[hp:2 * hp]
        ms = []
        for k in range(hp):
            col_part = jnp.where(ri <= ci, acr[k:k + 1, :], ncb[k:k + 1, :]).T
            row_part = jnp.where(ci <= ri, lgr[k:k + 1, :], lgr[hp + k:hp + k + 1, :])
            e = jnp.where(ci == ri, dsum[k:k + 1, :], jnp.exp2(col_part + row_part))
            ms.append((g * e).astype(BF16))
        mcat = jnp.concatenate(ms, axis=1)
        xc = x_ref[d["rows"], :]
        xstack = jnp.concatenate(
            [jnp.where(lanehead == k, xc, jnp.zeros_like(xc)) for k in range(hp)], axis=0)
        y = jnp.dot(mcat, xstack, preferred_element_type=F32)
        y = y + d["yoff_f"] + chb * d["scale_b"] + d_ref[0] * xc.astype(F32)
        y_ref[d["rows"], :] = y.astype(y_ref.dtype)
        return hb

    def bwd_trip(t, carry):
        chunks = [nc - 1 - (t * u + j) for j in range(u)]
        st = [None] * u
        st[0] = stage_a(chunks[0])
        if u > 1:
            st[1] = stage_a(chunks[1])
        stage_b(st[0])
        hb = h_ref[...]
        for j in range(u):
            if j + 2 < u:
                st[j + 2] = stage_a(chunks[j + 2])
            if j + 1 < u:
                stage_b(st[j + 1])
            hb = stage_c(st[j], hb)
            st[j] = None
        h_ref[...] = hb
        return carry

    lax.fori_loop(0, nc // u, bwd_trip, 0)


def _ssd_scan(xbc, dt_t, ac_t, tot_t, lg_t, d_lanes):
    l = xbc.shape[0]
    assert l % (SSM_CHUNK * SSD_CHUNKS_PER_TRIP) == 0
    nsteps = SSM_HEADS // HEADS_PER_STEP
    per_group = SSM_HEADS // SSM_GROUPS // HEADS_PER_STEP
    b_off = SSM_D_INNER // SSM_STATE
    c_off = b_off + SSM_GROUPS
    small = pl.BlockSpec((1, 2 * HEADS_PER_STEP, l), lambda j: (j, 0, 0))
    return pl.pallas_call(
        _ssd_kernel,
        grid=(nsteps,),
        in_specs=[pl.BlockSpec((l, SSD_LANES), lambda j: (0, j)),
                  pl.BlockSpec((l, SSM_STATE), lambda j: (0, b_off + j // per_group)),
                  pl.BlockSpec((l, SSM_STATE), lambda j: (0, c_off + j // per_group)),
                  small, small, small, small,
                  pl.BlockSpec((1, 1, SSD_LANES), lambda j: (j, 0, 0))],
        out_specs=pl.BlockSpec((l, SSD_LANES), lambda j: (0, j)),
        out_shape=jax.ShapeDtypeStruct((l, SSM_D_INNER), BF16),
        scratch_shapes=[pltpu.VMEM((l // SSM_CHUNK, SSM_STATE, SSD_LANES), BF16),
                        pltpu.VMEM((SSM_STATE, SSD_LANES), F32)],
        compiler_params=_cparams(1),
        name="ssd_scan",
    )(xbc, xbc, xbc, dt_t, ac_t, tot_t, lg_t, d_lanes)


def _na_window(case, qr):
    if case == 0:
        return 0, NA_WIN_H, -qr
    if case == 1:
        return qr, qr + NA_WIN_H, -(NA_WIN_H // 2) - qr
    return NA_Q_ROWS, NA_BAND_ROWS, -NA_WIN_H - qr


def _na_kernel(rpb_ref, q_ref, k_ref, v_ref, z_ref, o_ref, t_ref, bias_ref, *, rows):
    h = pl.program_id(0)
    nblk = rows // NA_Q_ROWS
    gw = GRID_W
    tq = NA_Q_ROWS * gw
    band = NA_BAND_ROWS * gw
    n_dr = 2 * NA_WIN_H - 1
    n_dc = 2 * NA_WIN_W - 1

    kc = lax.broadcasted_iota(jnp.int32, (gw, 2 * gw), 0)
    lane = lax.broadcasted_iota(jnp.int32, (gw, 2 * gw), 1)
    qc = lane % gw
    diff = kc - qc + (NA_WIN_W - 1)
    cs = jnp.clip(qc - NA_WIN_W // 2, 0, gw - NA_WIN_W)
    col_ok = (kc >= cs) & (kc < cs + NA_WIN_W)
    tables = [jnp.full((gw, 2 * gw), NEG_BIG, F32)] * n_dr
    for j in range(n_dc):
        on_diagonal = (diff == j) & col_ok
        for dr in range(n_dr):
            tables[dr] = jnp.where(on_diagonal, rpb_ref[(h * n_dr + dr) * n_dc + j] * LOG2E, tables[dr])
    for dr in range(n_dr):
        t_ref[dr] = tables[dr]
    neg = jnp.full((gw, 2 * gw), NEG_BIG, F32)
    for case in range(3):
        for kr in range(NA_BAND_ROWS):
            for pair in range(NA_Q_ROWS // 2):
                halves = []
                for qr in (2 * pair, 2 * pair + 1):
                    lo, hi, off = _na_window(case, qr)
                    halves.append(t_ref[kr + off + NA_WIN_H - 1] if lo <= kr < hi else neg)
                bias_ref[case, kr * gw:(kr + 1) * gw, pair * 2 * gw:(pair + 1) * 2 * gw] = (
                    jnp.where(lane < gw, halves[0], halves[1]))

    def scores(blk):
        _, q0, k0 = blk
        return lax.dot_general(k_ref[pl.ds(k0, band), :], q_ref[pl.ds(q0, tq), :],
                               (((1,), (1,)), ((), ())), preferred_element_type=F32)

    def finish(blk, s):
        case, q0, k0 = blk
        vb = v_ref[pl.ds(k0, band), :]
        ps, rden = [], []
        for pair in range(NA_Q_ROWS // 2):
            wins = [_na_window(case, qr) for qr in (2 * pair, 2 * pair + 1)]
            r0, r1 = min(w[0] for w in wins) * gw, max(w[1] for w in wins) * gw
            lanes = slice(pair * 2 * gw, (pair + 1) * 2 * gw)
            sq = s[r0:r1, lanes] + bias_ref[case, r0:r1, lanes]
            m = jnp.max(sq, axis=0, keepdims=True)
            p = jnp.exp2(sq - m)
            rden.append(1.0 / jnp.sum(p, axis=0, keepdims=True))
            parts = [p.astype(BF16)]
            if r0 > 0:
                parts.insert(0, jnp.zeros((r0, 2 * gw), BF16))
            if r1 < band:
                parts.append(jnp.zeros((band - r1, 2 * gw), BF16))
            ps.append(jnp.concatenate(parts, axis=0))
        o_t = lax.dot_general(vb, jnp.concatenate(ps, axis=1), (((0,), (0,)), ((), ())),
                              preferred_element_type=F32)
        o = (o_t * jnp.concatenate(rden, axis=1)).T
        z = z_ref[pl.ds(q0, tq), :].astype(F32)
        o_ref[pl.ds(q0, tq), :] = (o * (z * jax.nn.sigmoid(z))).astype(o_ref.dtype)

    def run_blocks(blocks):
        s = scores(blocks[0])
        for j, blk in enumerate(blocks):
            s_next = scores(blocks[j + 1]) if j + 1 < len(blocks) else None
            finish(blk, s)
            s = s_next

    def interior(i):
        q0 = i * tq
        if not isinstance(i, int):
            q0 = pl.multiple_of(q0, tq)
        return 1, q0, q0 - (NA_WIN_H // 2) * gw

    u = NA_BLOCKS_PER_TRIP
    top = (0, 0, 0)
    bottom = (2, (nblk - 1) * tq, (rows - NA_BAND_ROWS) * gw)
    run_blocks([top] + [interior(i) for i in range(1, u)])

    def trip(t, carry):
        run_blocks([interior(t * u + j) for j in range(u)])
        return carry

    lax.fori_loop(1, nblk // u - 1, trip, 0)
    run_blocks([interior(i) for i in range(nblk - u, nblk - 1)] + [bottom])


def _neighbourhood_attention(qkvz, rpb):
    l = qkvz.shape[0]
    rows = l // GRID_W
    nblk = rows // NA_Q_ROWS
    assert rows % NA_Q_ROWS == 0 and nblk % NA_BLOCKS_PER_TRIP == 0 and nblk >= 2 * NA_BLOCKS_PER_TRIP
    nh = NA_HEADS

    def col(part):
        return pl.BlockSpec((l, NA_HEAD_DIM), lambda h: (0, part * nh + h))

    return pl.pallas_call(
        functools.partial(_na_kernel, rows=rows),
        grid=(nh,),
        in_specs=[pl.BlockSpec(memory_space=pltpu.SMEM), col(0), col(1), col(2), col(3)],
        out_specs=col(0),
        out_shape=jax.ShapeDtypeStruct((l, NA_D_INNER), BF16),
        scratch_shapes=[pltpu.VMEM((2 * NA_WIN_H - 1, GRID_W, 2 * GRID_W), F32),
                        pltpu.VMEM((3, NA_BAND_ROWS * GRID_W, NA_Q_ROWS * GRID_W), F32)],
        compiler_params=_cparams(1),
        name="neighbourhood_attention",
    )(rpb.reshape(-1), qkvz, qkvz, qkvz, qkvz)


def _ssm_layer(h, norm_w, w_in, conv_w, conv_b, dt_bias, a_log, d_skip, gnorm_w, w_out):
    n_zx = SSM_D_INNER + SSM_CONV_DIM

    hp = HEADS_PER_STEP
    heads = jnp.arange(SSM_HEADS).reshape(SSM_HEADS // hp, 1, hp)
    perm = (heads + jnp.array([0, SSM_HEADS]).reshape(1, 2, 1)).reshape(-1)
    w_dt = w_in[:, n_zx:][:, perm]
    bias = dt_bias.reshape(-1)[perm].reshape(1, -1)
    alog = a_log.reshape(-1)[perm].reshape(-1, 1)
    u, *dt_rows = _norm_dt_project(h, norm_w, w_dt, bias, alog)
    zx = _matmul(u, w_in, n_zx, BF16, tm=2048, tn=1024)
    nsteps = SSM_HEADS // hp
    shp = (nsteps, 2 * hp, h.shape[0])

    xbc = _conv_silu(zx, conv_w, conv_b, SSM_D_INNER)
    d_lanes = jnp.repeat(d_skip, SSM_HEAD_DIM).reshape(nsteps, 1, SSD_LANES)
    y = _ssd_scan(xbc, *[r.reshape(shp) for r in dt_rows], d_lanes)
    return _gated_out_proj(y, zx, gnorm_w, w_out, h)


def _na_layer(h, norm_w, w_in, rpb, w_out):
    u = _rmsnorm(h, norm_w, BF16)
    n = w_in.shape[1]
    qscale = NA_HEAD_DIM ** -0.5 * LOG2E
    colscale = jnp.where(jnp.arange(n) < NA_D_INNER, qscale, 1.0).astype(F32).reshape(1, n)
    qkvz = _matmul(u, w_in, n, BF16, tm=2048, tn=1024, colscale=colscale)
    o = _neighbourhood_attention(qkvz, rpb)
    return _matmul(o, w_out, w_out.shape[1], F32, tm=512, tn=1024, res=h, single_buffer_w=True)


def kernel(x, norm_w, ssm_w_in, ssm_conv_w, ssm_conv_b, ssm_dt_bias, ssm_A_log, ssm_D, ssm_norm_w,
           ssm_w_out, na_w_in, na_rpb, na_w_out, final_norm_w):
    b, l, d = x.shape
    outs = []
    for bi in range(b):
        h = x[bi]
        h = _ssm_layer(h, norm_w[0], ssm_w_in[0], ssm_conv_w[0], ssm_conv_b[0], ssm_dt_bias[0],
                       ssm_A_log[0], ssm_D[0], ssm_norm_w[0], ssm_w_out[0])
        h = _na_layer(h, norm_w[1], na_w_in[0], na_rpb[0], na_w_out[0])
        outs.append(_rmsnorm(h, final_norm_w, x.dtype))
    return jnp.stack(outs, axis=0)
```

```python
import functools

import jax
import jax.numpy as jnp
from jax import lax
from jax.experimental import pallas as pl
from jax.experimental.pallas import tpu as pltpu

F32 = jnp.float32
BF16 = jnp.bfloat16

GRID_W = 64
SSM_HEAD_DIM = 64
SSM_HEADS = 64
SSM_GROUPS = 8
SSM_STATE = 128
SSM_CONV_W = 7
SSM_CHUNK = 128
SSM_D_INNER = SSM_HEADS * SSM_HEAD_DIM
SSM_CONV_DIM = SSM_D_INNER + 2 * SSM_GROUPS * SSM_STATE
NA_HEAD_DIM = 128
NA_HEADS = 32
NA_D_INNER = NA_HEADS * NA_HEAD_DIM
NA_WIN_H = 8
NA_WIN_W = 16
NORM_EPS = 1e-5

HEADS_PER_STEP = 4
SSD_LANES = HEADS_PER_STEP * SSM_HEAD_DIM
SSD_CHUNKS_PER_TRIP = 32
NA_Q_ROWS = 4
NA_BAND_ROWS = NA_Q_ROWS + NA_WIN_H
NA_BLOCKS_PER_TRIP = 32
LOG2E = 1.4426950408889634
NEG_BIG = -1e30
VMEM_LIMIT = 56 * 1024 * 1024


def _cparams(n_axes):
    return pltpu.CompilerParams(dimension_semantics=("arbitrary",) * n_axes,
                                vmem_limit_bytes=VMEM_LIMIT)


def _rmsnorm_kernel(x_ref, w_ref, o_ref):
    x = x_ref[...]
    ms = jnp.mean(x * x, axis=-1, keepdims=True)
    o_ref[...] = (x * lax.rsqrt(ms + NORM_EPS) * w_ref[...]).astype(o_ref.dtype)


def _rmsnorm(x, w, out_dtype, tm=1024):
    l, d = x.shape
    tm = min(tm, l)
    return pl.pallas_call(
        _rmsnorm_kernel,
        grid=(l // tm,),
        in_specs=[pl.BlockSpec((tm, d), lambda i: (i, 0)),
                  pl.BlockSpec((1, d), lambda i: (0, 0))],
        out_specs=pl.BlockSpec((tm, d), lambda i: (i, 0)),
        out_shape=jax.ShapeDtypeStruct((l, d), out_dtype),
        compiler_params=_cparams(1),
        name="rmsnorm",
    )(x, w.reshape(1, d))


GATED_K_SLICES = 8


def _gated_out_proj_kernel(y_ref, z_ref, w_ref, g_ref, r_ref, o_ref, wbf_ref):
    k = y_ref.shape[1]
    ks = k // GATED_K_SLICES

    @pl.when(pl.program_id(1) == 0)
    def _():
        wbf_ref[...] = w_ref[...].astype(BF16)

    acc = None
    ss = None
    for i in range(GATED_K_SLICES):
        cols = slice(i * ks, (i + 1) * ks)
        z = z_ref[:, cols].astype(F32)
        g = y_ref[:, cols].astype(F32) * (z * jax.nn.sigmoid(z))
        part = jnp.sum(g * g, axis=-1, keepdims=True)
        prod = jnp.dot((g * g_ref[:, cols]).astype(BF16), wbf_ref[cols, :], preferred_element_type=F32)
        acc = prod if acc is None else acc + prod
        ss = part if ss is None else ss + part
    o_ref[...] = r_ref[...] + acc * lax.rsqrt(ss * (1.0 / k) + NORM_EPS)


def _gated_out_proj(y, zx, gain, w, res, tm=512, tn=1024):
    l, d = y.shape
    n = w.shape[1]
    tm = min(tm, l)
    return pl.pallas_call(
        _gated_out_proj_kernel,
        grid=(n // tn, l // tm),
        in_specs=[pl.BlockSpec((tm, d), lambda j, i: (i, 0)),
                  pl.BlockSpec((tm, d), lambda j, i: (i, 0)),
                  pl.BlockSpec((d, tn), lambda j, i: (0, j), pipeline_mode=pl.Buffered(1)),
                  pl.BlockSpec((1, d), lambda j, i: (0, 0)),
                  pl.BlockSpec((tm, tn), lambda j, i: (i, j))],
        out_specs=pl.BlockSpec((tm, tn), lambda j, i: (i, j)),
        out_shape=jax.ShapeDtypeStruct((l, n), F32),
        scratch_shapes=[pltpu.VMEM((d, tn), BF16)],
        compiler_params=_cparams(2),
        name="gated_out_proj",
    )(y, zx, w, gain.reshape(1, d), res)


def _matmul_kernel(*refs, has_scale, has_res):
    x_ref, w_ref = refs[0], refs[1]
    pos = 2
    s_ref = r_ref = None
    if has_scale:
        s_ref = refs[pos]
        pos += 1
    if has_res:
        r_ref = refs[pos]
        pos += 1
    o_ref, wbf_ref = refs[pos], refs[pos + 1]

    @pl.when(pl.program_id(1) == 0)
    def _():
        wbf_ref[...] = w_ref[...].astype(BF16)

    acc = jnp.dot(x_ref[...], wbf_ref[...], preferred_element_type=F32)
    if has_scale:
        acc = acc * s_ref[...]
    if has_res:
        acc = acc + r_ref[...]
    o_ref[...] = acc.astype(o_ref.dtype)


def _matmul(x, w, n_out, out_dtype, tm, tn, colscale=None, res=None, single_buffer_w=False):
    l, k = x.shape
    tm = min(tm, l)
    assert l % tm == 0 and n_out % tn == 0
    w_mode = dict(pipeline_mode=pl.Buffered(1)) if single_buffer_w else {}
    in_specs = [pl.BlockSpec((tm, k), lambda n, m: (m, 0)),
                pl.BlockSpec((k, tn), lambda n, m: (0, n), **w_mode)]
    args = [x, w]
    if colscale is not None:
        in_specs.append(pl.BlockSpec((1, tn), lambda n, m: (0, n)))
        args.append(colscale)
    if res is not None:
        in_specs.append(pl.BlockSpec((tm, tn), lambda n, m: (m, n)))
        args.append(res)
    return pl.pallas_call(
        functools.partial(_matmul_kernel, has_scale=colscale is not None, has_res=res is not None),
        grid=(n_out // tn, l // tm),
        in_specs=in_specs,
        out_specs=pl.BlockSpec((tm, tn), lambda n, m: (m, n)),
        out_shape=jax.ShapeDtypeStruct((l, n_out), out_dtype),
        scratch_shapes=[pltpu.VMEM((k, tn), BF16)],
        compiler_params=_cparams(2),
        name="matmul",
    )(*args)


def _split3_dot(a, m_bf16):
    hi = a.astype(BF16)
    r1 = a - hi.astype(F32)
    mid = r1.astype(BF16)
    lo = (r1 - mid.astype(F32)).astype(BF16)
    out = jnp.dot(hi, m_bf16, preferred_element_type=F32)
    out = out + jnp.dot(mid, m_bf16, preferred_element_type=F32)
    return out + jnp.dot(lo, m_bf16, preferred_element_type=F32)


def _dt_kernel(x_ref, g_ref, w_ref, bias_ref, alog_ref, u_ref, dt_ref, ac_ref, tot_ref, lg_ref):
    tm = x_ref.shape[0]
    x = x_ref[...]
    u = (x * lax.rsqrt(jnp.mean(x * x, axis=-1, keepdims=True) + NORM_EPS) * g_ref[...]).astype(BF16)
    u_ref[...] = u
    raw = jnp.dot(u, w_ref[...].astype(BF16), preferred_element_type=F32)
    xb = raw + bias_ref[...]
    dt = jnp.maximum(xb, 0.0) + jnp.log1p(jnp.exp(-jnp.abs(xb)))
    dt_t = dt.T
    a = dt_t * (-jnp.exp(alog_ref[...]) * LOG2E)
    q = SSM_CHUNK
    si = lax.broadcasted_iota(jnp.int32, (q, q), 0)
    ti = lax.broadcasted_iota(jnp.int32, (q, q), 1)
    upper = jnp.where(si <= ti, 1.0, 0.0).astype(BF16)
    row = lax.broadcasted_iota(jnp.int32, (a.shape[0], q), 0)
    is_bwd = (row % (2 * HEADS_PER_STEP)) >= HEADS_PER_STEP
    dt_ref[...] = dt_t
    log2_dt = jnp.log2(dt_t)
    for c in range(tm // q):
        cols = slice(c * q, (c + 1) * q)
        a_c = a[:, cols]
        inc = _split3_dot(a_c, upper)
        tot_ref[:, cols] = inc
        ac = jnp.where(is_bwd, inc - a_c, inc)
        ac_ref[:, cols] = ac
        lg_ref[:, cols] = log2_dt[:, cols] + jnp.where(is_bwd, ac, -ac)


def _norm_dt_project(x, gain, w_dt, bias, alog, tm=1024):
    l, k = x.shape
    n = w_dt.shape[1]
    tm = min(tm, l)
    out = jax.ShapeDtypeStruct((n, l), F32)
    return pl.pallas_call(
        _dt_kernel,
        grid=(l // tm,),
        in_specs=[pl.BlockSpec((tm, k), lambda i: (i, 0)),
                  pl.BlockSpec((1, k), lambda i: (0, 0)),
                  pl.BlockSpec((k, n), lambda i: (0, 0)),
                  pl.BlockSpec((1, n), lambda i: (0, 0)),
                  pl.BlockSpec((n, 1), lambda i: (0, 0))],
        out_specs=[pl.BlockSpec((tm, k), lambda i: (i, 0))] + [pl.BlockSpec((n, tm), lambda i: (0, i))] * 4,
        out_shape=[jax.ShapeDtypeStruct((l, k), BF16)] + [out] * 4,
        compiler_params=_cparams(1),
        name="norm_dt_project",
    )(x, gain.reshape(1, k), w_dt, bias, alog)


def _conv_silu_kernel(x_ref, w_ref, b_ref, o_ref, *, tile):
    l, cw = x_ref.shape
    pad = 16
    half = SSM_CONV_W // 2
    n_tiles = l // tile
    ext = tile + 2 * pad
    w = w_ref[...]
    b = b_ref[...]
    taps = [j for j in range(SSM_CONV_W) if j != half]
    ti = lax.broadcasted_iota(jnp.int32, (len(taps) * tile, ext), 0)
    ri = lax.broadcasted_iota(jnp.int32, (len(taps) * tile, ext), 1)
    tap_of_row = jnp.zeros_like(ti)
    for n, j in enumerate(taps):
        tap_of_row = jnp.where(ti // tile == n, j, tap_of_row)
    shift = jnp.where(ri == ti % tile + pad - half + tap_of_row, 1.0, 0.0).astype(BF16)

    def shifted_rows(i):
        r0 = pl.multiple_of(i * tile, tile)
        p0 = pl.multiple_of(jnp.maximum(r0 - pad, 0), pad)
        n0 = pl.multiple_of(jnp.minimum(r0 + tile, l - pad), pad)
        zero = jnp.zeros((pad, cw), BF16)
        prev = jnp.where(i > 0, x_ref[pl.ds(p0, pad), :], zero)
        nxt = jnp.where(i < n_tiles - 1, x_ref[pl.ds(n0, pad), :], zero)
        xe = jnp.concatenate([prev, x_ref[pl.ds(r0, tile), :], nxt], axis=0)
        return jnp.dot(shift, xe, preferred_element_type=F32)

    def body(i, carry):
        shifted = shifted_rows(i)
        r0 = pl.multiple_of(i * tile, tile)
        acc = b + x_ref[pl.ds(r0, tile), :].astype(F32) * w[half:half + 1, :]
        for n, j in enumerate(taps):
            acc = acc + shifted[n * tile:(n + 1) * tile, :] * w[j:j + 1, :]
        o_ref[pl.ds(r0, tile), :] = (acc * jax.nn.sigmoid(acc)).astype(o_ref.dtype)
        return carry

    lax.fori_loop(0, n_tiles, body, 0, unroll=min(64, n_tiles))


def _conv_silu(zx, conv_w, conv_b, col0, cw=256, tile=128):
    l = zx.shape[0]
    c = conv_w.shape[1]
    off = col0 // cw
    return pl.pallas_call(
        functools.partial(_conv_silu_kernel, tile=tile),
        grid=(c // cw,),
        in_specs=[pl.BlockSpec((l, cw), lambda j: (0, off + j)),
                  pl.BlockSpec((SSM_CONV_W, cw), lambda j: (0, j)),
                  pl.BlockSpec((1, cw), lambda j: (0, j))],
        out_specs=pl.BlockSpec((l, cw), lambda j: (0, j)),
        out_shape=jax.ShapeDtypeStruct((l, c), BF16),
        compiler_params=_cparams(1),
        name="conv_silu",
    )(zx, conv_w, conv_b.reshape(1, c))


def _head_selector(first_rows):
    n = len(first_rows)
    shape = (3 * 2 * HEADS_PER_STEP * n, SSD_LANES * n)
    row = lax.broadcasted_iota(jnp.int32, shape, 0)
    lane = lax.broadcasted_iota(jnp.int32, shape, 1)
    first = jnp.zeros(shape, jnp.int32)
    for i, f in enumerate(first_rows):
        first = jnp.where(lane // SSD_LANES == i, f, first)
    hit = (row // (3 * 2 * HEADS_PER_STEP) == lane // SSD_LANES) & (
        row % (2 * HEADS_PER_STEP) == (lane % SSD_LANES) // SSM_HEAD_DIM + first)
    return jnp.where(hit, 1.0, 0.0).astype(BF16)


def _rows_to_lanes(blocks, selector):
    pieces = []
    for r8 in blocks:
        hi = r8.astype(BF16).astype(F32)
        r1 = r8 - hi
        mid = r1.astype(BF16).astype(F32)
        pieces += [hi, mid, r1 - mid]
    lhs = jnp.concatenate(pieces, axis=0).astype(BF16)
    return lax.dot_general(lhs, selector, (((0,), (0,)), ((), ())), preferred_element_type=F32)


def _head_lane_vector(c4, lanehead):
    out = jnp.broadcast_to(c4[0:1, :], (1, SSD_LANES))
    for k in range(1, HEADS_PER_STEP):
        out = jnp.where(lanehead == k, jnp.broadcast_to(c4[k:k + 1, :], (1, SSD_LANES)), out)
    return out


def _ssd_kernel(x_ref, b_ref, c_ref, dt_ref, ac_ref, tot_ref, lg_ref, d_ref, y_ref, hf_ref, h_ref):
    l = x_ref.shape[0]
    q = SSM_CHUNK
    nc = l // q
    hp = HEADS_PER_STEP
    lanehead = lax.broadcasted_iota(jnp.int32, (1, SSD_LANES), 1) // SSM_HEAD_DIM
    ri = lax.broadcasted_iota(jnp.int32, (q, q), 0)
    ci = lax.broadcasted_iota(jnp.int32, (q, q), 1)
    contract0 = (((0,), (0,)), ((), ()))
    contract_last = (((1,), (1,)), ((), ()))
    sel_f = _head_selector([0])
    sel_fbb = _head_selector([0, hp, hp])
    u = SSD_CHUNKS_PER_TRIP

    def chunk_rows(c):
        return pl.ds(pl.multiple_of(c * q, q), q)


    h_ref[...] = jnp.zeros_like(h_ref)

    def fwd_trip(t, carry):
        chunks = [t * u + j for j in range(u)]
        a_lasts, wfs, ss = [], [], []
        for c in chunks:
            rows = chunk_rows(c)
            acr = ac_ref[0, :, rows]
            a_lasts.append(acr[:, q - 1:q])
            wfs.append(_rows_to_lanes([jnp.exp2(a_lasts[-1] - acr) * dt_ref[0, :, rows]], sel_f))
        for c, wf in zip(chunks, wfs):
            rows = chunk_rows(c)
            xw = (x_ref[rows, :].astype(F32) * wf).astype(BF16)
            ss.append(lax.dot_general(b_ref[rows, :], xw, contract0, preferred_element_type=F32))
        h = h_ref[...]
        for c, a_last, s in zip(chunks, a_lasts, ss):
            hf_ref[c] = h.astype(BF16)
            h = h * _head_lane_vector(jnp.exp2(a_last[0:hp]), lanehead) + s
        h_ref[...] = h
        return carry

    lax.fori_loop(0, nc // u, fwd_trip, 0)

    h_ref[...] = jnp.zeros_like(h_ref)

    def stage_a(c):
        rows = chunk_rows(c)
        bc, cc = b_ref[rows, :], c_ref[rows, :]
        dtr, acr = dt_ref[0, :, rows], ac_ref[0, :, rows]
        t_last = tot_ref[0, :, rows][:, q - 1:q]
        g = lax.dot_general(cc, bc, contract_last, preferred_element_type=F32)
        scales = _rows_to_lanes([jnp.exp2(acr), jnp.exp2(t_last - acr), jnp.exp2(acr) * dtr], sel_fbb)
        yoff_f = jnp.dot(cc, hf_ref[c], preferred_element_type=F32) * scales[:, 0:SSD_LANES]
        return dict(rows=rows, bc=bc, cc=cc, dtr=dtr, acr=acr, t_last=t_last, g=g, yoff_f=yoff_f,
                    scale_b=scales[:, SSD_LANES:2 * SSD_LANES], wb=scales[:, 2 * SSD_LANES:])

    def stage_b(d):
        xw = (x_ref[d["rows"], :].astype(F32) * d["wb"]).astype(BF16)
        d["s"] = lax.dot_general(d["bc"], xw, contract0, preferred_element_type=F32)

    def stage_c(d, hb):
        chb = jnp.dot(d["cc"], hb.astype(BF16), preferred_element_type=F32)
        hb = hb * _head_lane_vector(jnp.exp2(d["t_last"][hp:2 * hp]), lanehead) + d["s"]
        dtr, acr, g = d["dtr"], d["acr"], d["g"]
        lgr = lg_ref[0, :, d["rows"]]
        ncb = -acr[hp:2 * hp]
        dsum = dtr[0:hp] + dtr[hp:2 * hp]
        ms = []
        for k in range(hp):
            col_part = jnp.where(ri <= ci, acr[k:k + 1, :], ncb[k:k + 1, :]).T
            row_part = jnp.where(ci <= ri, lgr[k:k + 1, :], lgr[hp + k:hp + k + 1, :])
            e = jnp.where(ci == ri, dsum[k:k + 1, :], jnp.exp2(col_part + row_part))
            ms.append((g * e).astype(BF16))
        mcat = jnp.concatenate(ms, axis=1)
        xc = x_ref[d["rows"], :]
        xstack = jnp.concatenate(
            [jnp.where(lanehead == k, xc, jnp.zeros_like(xc)) for k in range(hp)], axis=0)
        y = jnp.dot(mcat, xstack, preferred_element_type=F32)
        y = y + d["yoff_f"] + chb * d["scale_b"] + d_ref[0] * xc.astype(F32)
        y_ref[d["rows"], :] = y.astype(y_ref.dtype)
        return hb

    def bwd_trip(t, carry):
        chunks = [nc - 1 - (t * u + j) for j in range(u)]
        st = [None] * u
        st[0] = stage_a(chunks[0])
        if u > 1:
            st[1] = stage_a(chunks[1])
        stage_b(st[0])
        hb = h_ref[...]
        for j in range(u):
            if j + 2 < u:
                st[j + 2] = stage_a(chunks[j + 2])
            if j + 1 < u:
                stage_b(st[j + 1])
            hb = stage_c(st[j], hb)
            st[j] = None
        h_ref[...] = hb
        return carry

    lax.fori_loop(0, nc // u, bwd_trip, 0)


def _ssd_scan(xbc, dt_t, ac_t, tot_t, lg_t, d_lanes):
    l = xbc.shape[0]
    assert l % (SSM_CHUNK * SSD_CHUNKS_PER_TRIP) == 0
    nsteps = SSM_HEADS // HEADS_PER_STEP
    per_group = SSM_HEADS // SSM_GROUPS // HEADS_PER_STEP
    b_off = SSM_D_INNER // SSM_STATE
    c_off = b_off + SSM_GROUPS
    small = pl.BlockSpec((1, 2 * HEADS_PER_STEP, l), lambda j: (j, 0, 0))
    return pl.pallas_call(
        _ssd_kernel,
        grid=(nsteps,),
        in_specs=[pl.BlockSpec((l, SSD_LANES), lambda j: (0, j)),
                  pl.BlockSpec((l, SSM_STATE), lambda j: (0, b_off + j // per_group)),
                  pl.BlockSpec((l, SSM_STATE), lambda j: (0, c_off + j // per_group)),
                  small, small, small, small,
                  pl.BlockSpec((1, 1, SSD_LANES), lambda j: (j, 0, 0))],
        out_specs=pl.BlockSpec((l, SSD_LANES), lambda j: (0, j)),
        out_shape=jax.ShapeDtypeStruct((l, SSM_D_INNER), BF16),
        scratch_shapes=[pltpu.VMEM((l // SSM_CHUNK, SSM_STATE, SSD_LANES), BF16),
                        pltpu.VMEM((SSM_STATE, SSD_LANES), F32)],
        compiler_params=_cparams(1),
        name="ssd_scan",
    )(xbc, xbc, xbc, dt_t, ac_t, tot_t, lg_t, d_lanes)


def _na_window(case, qr):
    if case == 0:
        return 0, NA_WIN_H, -qr
    if case == 1:
        return qr, qr + NA_WIN_H, -(NA_WIN_H // 2) - qr
    return NA_Q_ROWS, NA_BAND_ROWS, -NA_WIN_H - qr


def _na_kernel(rpb_ref, q_ref, k_ref, v_ref, z_ref, o_ref, t_ref, bias_ref, *, rows):
    h = pl.program_id(0)
    nblk = rows // NA_Q_ROWS
    gw = GRID_W
    tq = NA_Q_ROWS * gw
    band = NA_BAND_ROWS * gw
    n_dr = 2 * NA_WIN_H - 1
    n_dc = 2 * NA_WIN_W - 1

    kc = lax.broadcasted_iota(jnp.int32, (gw, 2 * gw), 0)
    lane = lax.broadcasted_iota(jnp.int32, (gw, 2 * gw), 1)
    qc = lane % gw
    diff = kc - qc + (NA_WIN_W - 1)
    cs = jnp.clip(qc - NA_WIN_W // 2, 0, gw - NA_WIN_W)
    col_ok = (kc >= cs) & (kc < cs + NA_WIN_W)
    tables = [jnp.full((gw, 2 * gw), NEG_BIG, F32)] * n_dr
    for j in range(n_dc):
        on_diagonal = (diff == j) & col_ok
        for dr in range(n_dr):
            tables[dr] = jnp.where(on_diagonal, rpb_ref[(h * n_dr + dr) * n_dc + j] * LOG2E, tables[dr])
    for dr in range(n_dr):
        t_ref[dr] = tables[dr]
    neg = jnp.full((gw, 2 * gw), NEG_BIG, F32)
    for case in range(3):
        for kr in range(NA_BAND_ROWS):
            for pair in range(NA_Q_ROWS // 2):
                halves = []
                for qr in (2 * pair, 2 * pair + 1):
                    lo, hi, off = _na_window(case, qr)
                    halves.append(t_ref[kr + off + NA_WIN_H - 1] if lo <= kr < hi else neg)
                bias_ref[case, kr * gw:(kr + 1) * gw, pair * 2 * gw:(pair + 1) * 2 * gw] = (
                    jnp.where(lane < gw, halves[0], halves[1]))

    def scores(blk):
        _, q0, k0 = blk
        return lax.dot_general(k_ref[pl.ds(k0, band), :], q_ref[pl.ds(q0, tq), :],
                               (((1,), (1,)), ((), ())), preferred_element_type=F32)

    def finish(blk, s):
        case, q0, k0 = blk
        vb = v_ref[pl.ds(k0, band), :]
        ps, rden = [], []
        for pair in range(NA_Q_ROWS // 2):
            wins = [_na_window(case, qr) for qr in (2 * pair, 2 * pair + 1)]
            r0, r1 = min(w[0] for w in wins) * gw, max(w[1] for w in wins) * gw
            lanes = slice(pair * 2 * gw, (pair + 1) * 2 * gw)
            sq = s[r0:r1, lanes] + bias_ref[case, r0:r1, lanes]
            m = jnp.max(sq, axis=0, keepdims=True)
            p = jnp.exp2(sq - m)
            rden.append(1.0 / jnp.sum(p, axis=0, keepdims=True))
            parts = [p.astype(BF16)]
            if r0 > 0:
                parts.insert(0, jnp.zeros((r0, 2 * gw), BF16))
            if r1 < band:
                parts.append(jnp.zeros((band - r1, 2 * gw), BF16))
            ps.append(jnp.concatenate(parts, axis=0))
        o_t = lax.dot_general(vb, jnp.concatenate(ps, axis=1), (((0,), (0,)), ((), ())),
                              preferred_element_type=F32)
        o = (o_t * jnp.concatenate(rden, axis=1)).T
        z = z_ref[pl.ds(q0, tq), :].astype(F32)
        o_ref[pl.ds(q0, tq), :] = (o * (z * jax.nn.sigmoid(z))).astype(o_ref.dtype)

    def run_blocks(blocks):
        s = scores(blocks[0])
        for j, blk in enumerate(blocks):
            s_next = scores(blocks[j + 1]) if j + 1 < len(blocks) else None
            finish(blk, s)
            s = s_next

    def interior(i):
        q0 = i * tq
        if not isinstance(i, int):
            q0 = pl.multiple_of(q0, tq)
        return 1, q0, q0 - (NA_WIN_H // 2) * gw

    u = NA_BLOCKS_PER_TRIP
    top = (0, 0, 0)
    bottom = (2, (nblk - 1) * tq, (rows - NA_BAND_ROWS) * gw)
    if nblk == u:
        run_blocks([top] + [interior(i) for i in range(1, nblk - 1)] + [bottom])
        return
    run_blocks([top] + [interior(i) for i in range(1, u)])

    def trip(t, carry):
        run_blocks([interior(t * u + j) for j in range(u)])
        return carry

    lax.fori_loop(1, nblk // u - 1, trip, 0)
    run_blocks([interior(i) for i in range(nblk - u, nblk - 1)] + [bottom])


def _neighbourhood_attention(qkvz, rpb):
    l = qkvz.shape[0]
    rows = l // GRID_W
    nblk = rows // NA_Q_ROWS
    assert rows % NA_Q_ROWS == 0 and nblk % NA_BLOCKS_PER_TRIP == 0 and rows >= NA_BAND_ROWS
    nh = NA_HEADS

    def col(part):
        return pl.BlockSpec((l, NA_HEAD_DIM), lambda h: (0, part * nh + h))

    return pl.pallas_call(
        functools.partial(_na_kernel, rows=rows),
        grid=(nh,),
        in_specs=[pl.BlockSpec(memory_space=pltpu.SMEM), col(0), col(1), col(2), col(3)],
        out_specs=col(0),
        out_shape=jax.ShapeDtypeStruct((l, NA_D_INNER), BF16),
        scratch_shapes=[pltpu.VMEM((2 * NA_WIN_H - 1, GRID_W, 2 * GRID_W), F32),
                        pltpu.VMEM((3, NA_BAND_ROWS * GRID_W, NA_Q_ROWS * GRID_W), F32)],
        compiler_params=_cparams(1),
        name="neighbourhood_attention",
    )(rpb.reshape(-1), qkvz, qkvz, qkvz, qkvz)


def _ssm_layer(h, norm_w, w_in, conv_w, conv_b, dt_bias, a_log, d_skip, gnorm_w, w_out):
    n_zx = SSM_D_INNER + SSM_CONV_DIM

    hp = HEADS_PER_STEP
    heads = jnp.arange(SSM_HEADS).reshape(SSM_HEADS // hp, 1, hp)
    perm = (heads + jnp.array([0, SSM_HEADS]).reshape(1, 2, 1)).reshape(-1)
    w_dt = w_in[:, n_zx:][:, perm]
    bias = dt_bias.reshape(-1)[perm].reshape(1, -1)
    alog = a_log.reshape(-1)[perm].reshape(-1, 1)
    u, *dt_rows = _norm_dt_project(h, norm_w, w_dt, bias, alog)
    zx = _matmul(u, w_in, n_zx, BF16, tm=2048, tn=1024)
    nsteps = SSM_HEADS // hp
    shp = (nsteps, 2 * hp, h.shape[0])

    xbc = _conv_silu(zx, conv_w, conv_b, SSM_D_INNER)
    d_lanes = jnp.repeat(d_skip, SSM_HEAD_DIM).reshape(nsteps, 1, SSD_LANES)
    y = _ssd_scan(xbc, *[r.reshape(shp) for r in dt_rows], d_lanes)
    return _gated_out_proj(y, zx, gnorm_w, w_out, h)


def _na_layer(h, norm_w, w_in, rpb, w_out):
    u = _rmsnorm(h, norm_w, BF16)
    n = w_in.shape[1]
    qscale = NA_HEAD_DIM ** -0.5 * LOG2E
    colscale = jnp.where(jnp.arange(n) < NA_D_INNER, qscale, 1.0).astype(F32).reshape(1, n)
    qkvz = _matmul(u, w_in, n, BF16, tm=2048, tn=1024, colscale=colscale)
    o = _neighbourhood_attention(qkvz, rpb)
    return _matmul(o, w_out, w_out.shape[1], F32, tm=512, tn=1024, res=h, single_buffer_w=True)


def kernel(x, norm_w, ssm_w_in, ssm_conv_w, ssm_conv_b, ssm_dt_bias, ssm_A_log, ssm_D, ssm_norm_w,
           ssm_w_out, na_w_in, na_rpb, na_w_out, final_norm_w):
    b, l, d = x.shape
    outs = []
    for bi in range(b):
        h = x[bi]
        h = _ssm_layer(h, norm_w[0], ssm_w_in[0], ssm_conv_w[0], ssm_conv_b[0], ssm_dt_bias[0],
                       ssm_A_log[0], ssm_D[0], ssm_norm_w[0], ssm_w_out[0])
        h = _na_layer(h, norm_w[1], na_w_in[0], na_rpb[0], na_w_out[0])
        outs.append(_rmsnorm(h, final_norm_w, x.dtype))
    return jnp.stack(outs, axis=0)
```
